```python
import jax
import jax.numpy as jnp
from jax import lax
import numpy as np

D_MODEL = 2048
BATCH = 2
SEQ = 4096
DEPTH = 2
DEC_BATCH = 128
DEC_SEQ = 1
PAST_LEN = 16384
PAGE_SIZE = 128

D_FF = 5632
CHUNK = 128
A_GROUPS = 4
A_GROUP_W = 128
A_WIDTH = A_GROUPS * A_GROUP_W
MOBA_HEADS = 8
MOBA_KV_HEADS = 2
MOBA_GROUP = MOBA_HEADS // MOBA_KV_HEADS
MOBA_HEAD_DIM = 64
MOBA_BLOCK = 256
MOBA_TOPK = 3
MOBA_QBLK = 64
MLA_HEADS = 8
MLA_NOPE = 64
MLA_ROPE = 32
MLA_V = 64
MLA_Q_LORA = 384
MLA_KV_LORA = 128
MLA_QBLK = 128
MLA_SCALE = (MLA_NOPE + MLA_ROPE) ** -0.5
ROPE_BASE = 10000.0
N_BRANCH = 3
IN_SPLITS = (2 * A_WIDTH, MOBA_HEADS * MOBA_HEAD_DIM, MOBA_KV_HEADS * MOBA_HEAD_DIM,
             MOBA_KV_HEADS * MOBA_HEAD_DIM, MLA_Q_LORA, MLA_KV_LORA, MLA_ROPE,
             N_BRANCH * D_MODEL)
D_IN = 2 * A_WIDTH + (MOBA_HEADS + 2 * MOBA_KV_HEADS) * MOBA_HEAD_DIM + MLA_Q_LORA + MLA_KV_LORA + MLA_ROPE + N_BRANCH * D_MODEL
EPS = 1e-6
NEG = -1e30

kernel_name = 'hybrid_gmlp_moba_mla_decode_step'


def rmsnorm(x, g):
    x32 = x.astype(jnp.float32)
    y = x32 * lax.rsqrt(jnp.mean(x32 * x32, axis=-1, keepdims=True) + EPS)
    return (y * g.astype(jnp.float32)).astype(x.dtype)


def layernorm(x, g, b):
    x32 = x.astype(jnp.float32)
    mu = jnp.mean(x32, axis=-1, keepdims=True)
    xc = x32 - mu
    y = xc * lax.rsqrt(jnp.mean(xc * xc, axis=-1, keepdims=True) + EPS)
    return (y * g.astype(jnp.float32) + b.astype(jnp.float32)).astype(x.dtype)


def half_ffn(h, pre_g, w_in, w_out, post_g):
    gate, up = jnp.split(rmsnorm(h, pre_g) @ w_in, 2, axis=-1)
    return h + 0.5 * rmsnorm((jax.nn.silu(gate) * up) @ w_out, post_g)


def rope(x, pos):
    half = MLA_ROPE // 2
    inv = ROPE_BASE ** (-jnp.arange(half, dtype=jnp.float32) / half)
    ang = pos.astype(jnp.float32)[:, None] * inv
    ang = ang.reshape((ang.shape[0],) + (1,) * (x.ndim - 3) + (half,))
    cos, sin = jnp.cos(ang), jnp.sin(ang)
    x32 = x.astype(jnp.float32)
    x1, x2 = x32[..., :half], x32[..., half:]
    return jnp.concatenate([x1 * cos - x2 * sin, x2 * cos + x1 * sin], axis=-1).astype(x.dtype)


def split_in(z):
    offs = np.cumsum(np.array(IN_SPLITS))[:-1].tolist()
    return jnp.split(z, offs, axis=-1)


def sweep_queries(fn, qs, qpos, qblk):
    B, L = qs[0].shape[:2]
    n = L // qblk
    blocks = tuple(jnp.moveaxis(a.reshape((B, n, qblk) + a.shape[2:]), 1, 0) for a in qs)
    out = lax.map(lambda args: fn(*args), blocks + (qpos.reshape(n, qblk),))
    out = jnp.moveaxis(out, 0, 1)
    return out.reshape((B, L) + out.shape[3:])


def kv_head_index():
    return (jnp.arange(MOBA_HEADS) // MOBA_GROUP)[None, None, :, None, None]


def chunk_mlp_branch(z_uv, ln_g, ln_b, w_s, b_s, w_proj):
    uv = jax.nn.gelu(z_uv)
    u, v = jnp.split(uv, 2, axis=-1)
    v = layernorm(v, ln_g, ln_b)
    B, L, _ = v.shape
    n_chunks = -(-L // CHUNK)
    vp = jnp.pad(v, ((0, 0), (0, n_chunks * CHUNK - L), (0, 0)))
    vp = vp.reshape(B, n_chunks, CHUNK, A_GROUPS, A_GROUP_W)
    causal = jnp.tril(jnp.ones((CHUNK, CHUNK), dtype=bool))
    ws = jnp.where(causal, w_s, 0)
    mixed = jnp.einsum('gts,bnsgc->bntgc', ws, vp) + b_s.T[None, None, :, :, None]
    mixed = mixed.reshape(B, n_chunks * CHUNK, A_WIDTH)[:, :L]
    return (u * mixed) @ w_proj, v


def moba_block_attend(q, qpos, block_means, fetch):
    B, Lq = q.shape[:2]
    nb = block_means.shape[1]
    q32 = q.astype(jnp.float32)
    qg = q32.reshape(B, Lq, MOBA_KV_HEADS, MOBA_GROUP, MOBA_HEAD_DIM)
    s = jnp.einsum('blgrd,bngd->blgrn', qg, block_means).reshape(B, Lq, MOBA_HEADS, nb)
    own = qpos // MOBA_BLOCK
    fully_past = jnp.arange(nb)[None, :] < own[:, None]
    s = jnp.where(fully_past[None, :, None, :], s, NEG)
    _, top = lax.top_k(s, min(MOBA_TOPK, nb))
    top_ok = top < own[None, :, None, None]
    own_b = jnp.broadcast_to(own[None, :, None, None], top.shape[:-1] + (1,))
    blocks = jnp.concatenate([top, own_b], axis=-1)
    blk_ok = jnp.concatenate([top_ok, jnp.ones(own_b.shape, dtype=bool)], axis=-1)
    pos = blocks[..., None] * MOBA_BLOCK + jnp.arange(MOBA_BLOCK)
    ok = blk_ok[..., None] & (pos <= qpos[None, :, None, None, None])
    k, v = fetch(pos)
    logits = jnp.einsum('blhd,blhjrd->blhjr', q32, k.astype(jnp.float32)) * (MOBA_HEAD_DIM ** -0.5)
    logits = jnp.where(ok, logits, NEG)
    p = jax.nn.softmax(logits.reshape(B, Lq, MOBA_HEADS, -1), axis=-1).reshape(logits.shape)
    out = jnp.einsum('blhjr,blhjrd->blhd', p, v.astype(jnp.float32))
    return out.astype(q.dtype)


def moba_prompt(q, k, v):
    B, S = q.shape[:2]
    nb = -(-S // MOBA_BLOCK)
    kpad = jnp.pad(k.astype(jnp.float32), ((0, 0), (0, nb * MOBA_BLOCK - S), (0, 0), (0, 0)))
    means = jnp.mean(kpad.reshape(B, nb, MOBA_BLOCK, MOBA_KV_HEADS, MOBA_HEAD_DIM), axis=2)
    b_idx = jnp.arange(B)[:, None, None, None, None]
    kvh = kv_head_index()

    def fetch(pos):
        pp = jnp.clip(pos, 0, S - 1)
        return k[b_idx, pp, kvh], v[b_idx, pp, kvh]

    attend = lambda qb, pb: moba_block_attend(qb, pb, means, fetch)
    return sweep_queries(attend, (q,), jnp.arange(S, dtype=jnp.int32), MOBA_QBLK)


def moba_sample(q, k_new, v_new, pool_k, pool_v, page_table):
    DB, Lq = q.shape[:2]
    n_pages = page_table.shape[1]
    past = n_pages * PAGE_SIZE
    nb = -(-(past + Lq) // MOBA_BLOCK)
    page_sum = jnp.sum(pool_k[page_table].astype(jnp.float32), axis=2)
    oh_page = jax.nn.one_hot((jnp.arange(n_pages) * PAGE_SIZE) // MOBA_BLOCK, nb, dtype=jnp.float32)
    oh_new = jax.nn.one_hot((past + jnp.arange(Lq)) // MOBA_BLOCK, nb, dtype=jnp.float32)
    means = (jnp.einsum('bpgd,pn->bngd', page_sum, oh_page)
             + jnp.einsum('blgd,ln->bngd', k_new.astype(jnp.float32), oh_new)) / MOBA_BLOCK
    flat_k = pool_k.reshape((-1,) + pool_k.shape[2:])
    flat_v = pool_v.reshape((-1,) + pool_v.shape[2:])
    b_idx = jnp.arange(DB)[:, None, None, None, None]
    kvh = kv_head_index()

    def fetch(pos):
        in_past = (pos < past)[..., None]
        pp = jnp.clip(pos, 0, past - 1)
        phys = page_table[b_idx, pp // PAGE_SIZE] * PAGE_SIZE + pp % PAGE_SIZE
        pn = jnp.clip(pos - past, 0, Lq - 1)
        kk = jnp.where(in_past, flat_k[phys, kvh], k_new[b_idx, pn, kvh])
        vv = jnp.where(in_past, flat_v[phys, kvh], v_new[b_idx, pn, kvh])
        return kk, vv

    attend = lambda qb, pb: moba_block_attend(qb, pb, means, fetch)
    return sweep_queries(attend, (q,), past + jnp.arange(Lq, dtype=jnp.int32), 1)


def mla_project(z_cq, z_ckv, z_kr, pos, q_norm_g, w_uq, kv_norm_g, w_uk):
    B, L = z_cq.shape[:2]
    c_q = rmsnorm(z_cq, q_norm_g)
    q = (c_q @ w_uq).reshape(B, L, MLA_HEADS, MLA_NOPE + MLA_ROPE)
    q_nope, q_rope = q[..., :MLA_NOPE], rope(q[..., MLA_NOPE:], pos)
    q_lat = jnp.einsum('blhn,chn->blhc', q_nope, w_uk)
    c_kv = rmsnorm(z_ckv, kv_norm_g)
    k_rope = rope(z_kr, pos)
    return q_lat, q_rope, c_kv, k_rope


def mla_prompt(q_lat, q_rope, c_kv, k_rope):
    S = c_kv.shape[1]
    ckv = c_kv.astype(jnp.float32)
    kr = k_rope.astype(jnp.float32)
    kpos = jnp.arange(S, dtype=jnp.int32)

    def blk(ql, qr, qpos):
        s = (jnp.einsum('bqhc,btc->bqht', ql.astype(jnp.float32), ckv)
             + jnp.einsum('bqhr,btr->bqht', qr.astype(jnp.float32), kr)) * MLA_SCALE
        s = jnp.where(kpos[None, None, None, :] <= qpos[None, :, None, None], s, NEG)
        p = jax.nn.softmax(s, axis=-1)
        return jnp.einsum('bqht,btc->bqhc', p, ckv).astype(ql.dtype)

    return sweep_queries(blk, (q_lat, q_rope), kpos, MLA_QBLK)


def mla_sample(q_lat, q_rope, c_new, kr_new, pool_ckv, pool_kr, page_table):
    DB, Lq = q_lat.shape[:2]
    c_past = pool_ckv[page_table].reshape(DB, -1, MLA_KV_LORA).astype(jnp.float32)
    kr_past = pool_kr[page_table].reshape(DB, -1, MLA_ROPE).astype(jnp.float32)
    ql = q_lat.astype(jnp.float32)
    qr = q_rope.astype(jnp.float32)
    cn = c_new.astype(jnp.float32)
    krn = kr_new.astype(jnp.float32)
    s_past = (jnp.einsum('blhc,btc->blht', ql, c_past) + jnp.einsum('blhr,btr->blht', qr, kr_past)) * MLA_SCALE
    s_new = (jnp.einsum('blhc,bmc->blhm', ql, cn) + jnp.einsum('blhr,bmr->blhm', qr, krn)) * MLA_SCALE
    causal = jnp.arange(Lq)[None, :] <= jnp.arange(Lq)[:, None]
    s_new = jnp.where(causal[None, :, None, :], s_new, NEG)
    p = jax.nn.softmax(jnp.concatenate([s_past, s_new], axis=-1), axis=-1)
    n_past = s_past.shape[-1]
    o = (jnp.einsum('blht,btc->blhc', p[..., :n_past], c_past)
         + jnp.einsum('blhm,bmc->blhc', p[..., n_past:], cn))
    return o.astype(q_lat.dtype)


def mixer(xn, pos, attend_moba, attend_mla, w_in, gate_b, sgu_ln_g, sgu_ln_b, sgu_w, sgu_b,
          sgu_proj, moba_proj, q_norm_g, w_uq, kv_norm_g, w_uk, w_uv, mla_proj, w_out):
    B, L, _ = xn.shape
    z_uv, z_q, z_k, z_v, z_cq, z_ckv, z_kr, z_gate = split_in(xn @ w_in)
    y_a, v_rows = chunk_mlp_branch(z_uv, sgu_ln_g, sgu_ln_b, sgu_w, sgu_b, sgu_proj)
    q = z_q.reshape(B, L, MOBA_HEADS, MOBA_HEAD_DIM)
    k = z_k.reshape(B, L, MOBA_KV_HEADS, MOBA_HEAD_DIM)
    v = z_v.reshape(B, L, MOBA_KV_HEADS, MOBA_HEAD_DIM)
    y_b = attend_moba(q, k, v).reshape(B, L, -1) @ moba_proj
    q_lat, q_rope, c_kv, k_rope = mla_project(z_cq, z_ckv, z_kr, pos, q_norm_g, w_uq, kv_norm_g, w_uk)
    o_lat = attend_mla(q_lat, q_rope, c_kv, k_rope)
    y_c = jnp.einsum('blhc,chv->blhv', o_lat, w_uv).reshape(B, L, -1) @ mla_proj
    g = jax.nn.sigmoid((z_gate.reshape(B, L, N_BRANCH, D_MODEL) + gate_b).astype(jnp.float32)).astype(xn.dtype)
    merged = g[:, :, 0] * y_a + g[:, :, 1] * y_b + g[:, :, 2] * y_c
    return merged @ w_out, (v_rows, k, v, c_kv, k_rope)


def setup_inputs(seed: int = 0) -> dict:
    key = jax.random.key(seed)
    ks = iter(jax.random.split(key, 40))
    f32 = jnp.float32

    def nrm(shape, scale):
        return jax.random.normal(next(ks), shape, f32) * scale

    def gain(shape):
        return 1.0 + nrm(shape, 0.05)

    n_pages = PAST_LEN // PAGE_SIZE
    n_used = DEC_BATCH * n_pages
    n_pool = (n_used * 5) // 4
    out = {}
    out['x_prompt'] = nrm((BATCH, SEQ, D_MODEL), 1.0)
    out['x_sample'] = nrm((DEC_BATCH, DEC_SEQ, D_MODEL), 1.0)
    out['cache_moba_k'] = nrm((DEPTH, n_pool, PAGE_SIZE, MOBA_KV_HEADS, MOBA_HEAD_DIM), 1.0)
    out['cache_moba_v'] = nrm((DEPTH, n_pool, PAGE_SIZE, MOBA_KV_HEADS, MOBA_HEAD_DIM), 1.0)
    out['cache_mla_ckv'] = nrm((DEPTH, n_pool, PAGE_SIZE, MLA_KV_LORA), 1.0)
    out['cache_mla_krope'] = nrm((DEPTH, n_pool, PAGE_SIZE, MLA_ROPE), 1.0)
    perm = jax.random.permutation(next(ks), n_pool)
    out['page_table'] = perm[:n_used].reshape(DEC_BATCH, n_pages).astype(jnp.int32)
    out['ffn1_pre_g'] = gain((DEPTH, D_MODEL))
    out['ffn1_w_in'] = nrm((DEPTH, D_MODEL, 2 * D_FF), D_MODEL ** -0.5)
    out['ffn1_w_out'] = nrm((DEPTH, D_FF, D_MODEL), D_FF ** -0.5)
    out['ffn1_post_g'] = gain((DEPTH, D_MODEL))
    out['mix_pre_g'] = gain((DEPTH, D_MODEL))
    out['mix_w_in'] = nrm((DEPTH, D_MODEL, D_IN), D_MODEL ** -0.5)
    out['mix_gate_b'] = nrm((DEPTH, N_BRANCH, D_MODEL), 0.1)
    out['sgu_ln_g'] = gain((DEPTH, A_WIDTH))
    out['sgu_ln_b'] = nrm((DEPTH, A_WIDTH), 0.02)
    out['sgu_w'] = nrm((DEPTH, A_GROUPS, CHUNK, CHUNK), CHUNK ** -0.5)
    out['sgu_b'] = gain((DEPTH, A_GROUPS, CHUNK))
    out['sgu_proj'] = nrm((DEPTH, A_WIDTH, D_MODEL), A_WIDTH ** -0.5)
    out['moba_proj'] = nrm((DEPTH, MOBA_HEADS * MOBA_HEAD_DIM, D_MODEL), (MOBA_HEADS * MOBA_HEAD_DIM) ** -0.5)
    out['mla_q_norm_g'] = gain((DEPTH, MLA_Q_LORA))
    out['mla_w_uq'] = nrm((DEPTH, MLA_Q_LORA, MLA_HEADS * (MLA_NOPE + MLA_ROPE)), MLA_Q_LORA ** -0.5)
    out['mla_kv_norm_g'] = gain((DEPTH, MLA_KV_LORA))
    out['mla_w_uk'] = nrm((DEPTH, MLA_KV_LORA, MLA_HEADS, MLA_NOPE), MLA_KV_LORA ** -0.5)
    out['mla_w_uv'] = nrm((DEPTH, MLA_KV_LORA, MLA_HEADS, MLA_V), MLA_KV_LORA ** -0.5)
    out['mla_proj'] = nrm((DEPTH, MLA_HEADS * MLA_V, D_MODEL), (MLA_HEADS * MLA_V) ** -0.5)
    out['mix_w_out'] = nrm((DEPTH, D_MODEL, D_MODEL), D_MODEL ** -0.5)
    out['mix_post_g'] = gain((DEPTH, D_MODEL))
    out['ffn2_pre_g'] = gain((DEPTH, D_MODEL))
    out['ffn2_w_in'] = nrm((DEPTH, D_MODEL, 2 * D_FF), D_MODEL ** -0.5)
    out['ffn2_w_out'] = nrm((DEPTH, D_FF, D_MODEL), D_FF ** -0.5)
    out['ffn2_post_g'] = gain((DEPTH, D_MODEL))
    return out


def reference(x_prompt, x_sample, cache_moba_k, cache_moba_v, cache_mla_ckv, cache_mla_krope,
              page_table, ffn1_pre_g, ffn1_w_in, ffn1_w_out, ffn1_post_g, mix_pre_g, mix_w_in,
              mix_gate_b, sgu_ln_g, sgu_ln_b, sgu_w, sgu_b, sgu_proj, moba_proj, mla_q_norm_g,
              mla_w_uq, mla_kv_norm_g, mla_w_uk, mla_w_uv, mla_proj, mix_w_out, mix_post_g,
              ffn2_pre_g, ffn2_w_in, ffn2_w_out, ffn2_post_g):
    past = page_table.shape[1] * PAGE_SIZE
    pos_p = jnp.arange(x_prompt.shape[1], dtype=jnp.int32)
    pos_s = past + jnp.arange(x_sample.shape[1], dtype=jnp.int32)
    hp, hs = x_prompt, x_sample
    mk_p, mv_p, ck_p, kr_p = [], [], [], []
    mk_s, mv_s, ck_s, kr_s, cv_s = [], [], [], [], []
    for l in range(DEPTH):
        hp = half_ffn(hp, ffn1_pre_g[l], ffn1_w_in[l], ffn1_w_out[l], ffn1_post_g[l])
        hs = half_ffn(hs, ffn1_pre_g[l], ffn1_w_in[l], ffn1_w_out[l], ffn1_post_g[l])
        mw = (mix_w_in[l], mix_gate_b[l], sgu_ln_g[l], sgu_ln_b[l], sgu_w[l], sgu_b[l], sgu_proj[l],
              moba_proj[l], mla_q_norm_g[l], mla_w_uq[l], mla_kv_norm_g[l], mla_w_uk[l], mla_w_uv[l],
              mla_proj[l], mix_w_out[l])
        yp, (_, kp, vp, cp, rp) = mixer(rmsnorm(hp, mix_pre_g[l]), pos_p, moba_prompt, mla_prompt, *mw)
        hp = hp + rmsnorm(yp, mix_post_g[l])
        attend_moba_s = lambda q, k, v: moba_sample(q, k, v, cache_moba_k[l], cache_moba_v[l], page_table)
        attend_mla_s = lambda ql, qr, c, r: mla_sample(ql, qr, c, r, cache_mla_ckv[l], cache_mla_krope[l], page_table)
        ys, (vs_rows, ks_, vs_, cs_, rs_) = mixer(rmsnorm(hs, mix_pre_g[l]), pos_s, attend_moba_s, attend_mla_s, *mw)
        hs = hs + rmsnorm(ys, mix_post_g[l])
        hp = half_ffn(hp, ffn2_pre_g[l], ffn2_w_in[l], ffn2_w_out[l], ffn2_post_g[l])
        hs = half_ffn(hs, ffn2_pre_g[l], ffn2_w_in[l], ffn2_w_out[l], ffn2_post_g[l])
        mk_p.append(kp)
        mv_p.append(vp)
        ck_p.append(cp)
        kr_p.append(rp)
        mk_s.append(ks_)
        mv_s.append(vs_)
        ck_s.append(cs_)
        kr_s.append(rs_)
        cv_s.append(vs_rows)
    new_moba_k_prompt = jnp.stack(mk_p)
    new_moba_v_prompt = jnp.stack(mv_p)
    new_mla_ckv_prompt = jnp.stack(ck_p)
    new_mla_krope_prompt = jnp.stack(kr_p)
    new_moba_k_sample = jnp.stack(mk_s)
    new_moba_v_sample = jnp.stack(mv_s)
    new_mla_ckv_sample = jnp.stack(ck_s)
    new_mla_krope_sample = jnp.stack(kr_s)
    new_chunk_v_sample = jnp.stack(cv_s)
    return (hp, hs, new_moba_k_prompt, new_moba_v_prompt, new_mla_ckv_prompt, new_mla_krope_prompt,
            new_moba_k_sample, new_moba_v_sample, new_mla_ckv_sample, new_mla_krope_sample,
            new_chunk_v_sample)
```

```python
import functools

import numpy as np
import jax
import jax.numpy as jnp
from jax import lax
from jax.experimental import pallas as pl
from jax.experimental.pallas import tpu as pltpu

F32 = jnp.float32
BF16 = jnp.bfloat16

CHUNK = 128
A_GROUPS = 4
A_GROUP_W = 128
A_WIDTH = A_GROUPS * A_GROUP_W
MOBA_HEADS = 8
MOBA_KV_HEADS = 2
MOBA_GROUP = MOBA_HEADS // MOBA_KV_HEADS
MOBA_HEAD_DIM = 64
MOBA_BLOCK = 256
MOBA_TOPK = 3
MLA_HEADS = 8
MLA_NOPE = 64
MLA_ROPE = 32
MLA_V = 64
MLA_Q_LORA = 384
MLA_KV_LORA = 128
MLA_SCALE = (MLA_NOPE + MLA_ROPE) ** -0.5
MOBA_SCALE = MOBA_HEAD_DIM ** -0.5
ROPE_BASE = 10000.0
N_BRANCH = 3
EPS = 1e-6
NEG = -1e30

LANES = 128
VMEM_LIMIT = 56 * 1024 * 1024
MASK_BIG = 2.0 ** 100
PICK_NEG = -3.0e38

KV_W = MOBA_KV_HEADS * MOBA_HEAD_DIM
Q_W = MOBA_HEADS * MOBA_HEAD_DIM
OFF_UV = 0
OFF_Q = OFF_UV + 2 * A_WIDTH
OFF_CQ = OFF_Q + Q_W
OFF_CKV = OFF_CQ + MLA_Q_LORA
OFF_K = OFF_CKV + MLA_KV_LORA
OFF_V = OFF_K + KV_W
OFF_KR = OFF_V + KV_W
OFF_KRROT = OFF_KR + LANES
N_SMALL = OFF_KRROT + LANES
MLA_QW = 2 * LANES
SAMPLE_PAGES_PER_STEP = 16


def _cparams(sem, vmem=VMEM_LIMIT):
    return pltpu.CompilerParams(dimension_semantics=sem, vmem_limit_bytes=vmem)


def _pick_tile(n, cap, mult=LANES):
    best = None
    t = mult
    while t <= min(n, cap):
        if n % t == 0:
            best = t
        t += mult
    assert best is not None, (n, cap, mult)
    return best


def _rms(x, g):
    ms = jnp.mean(x * x, axis=-1, keepdims=True)
    return x * lax.rsqrt(ms + EPS) * g


def _sigmoid(x):
    return 1.0 / (1.0 + jnp.exp(-x))


def _dot(a, b):
    return jnp.dot(a, b, preferred_element_type=F32)


def _dot_nt(a, b, precision=None):
    return lax.dot_general(a, b, (((1,), (1,)), ((), ())), precision=precision,
                           preferred_element_type=F32)


def _ffn_kernel(h_ref, pre_ref, wg_ref, wu_ref, wo_ref, post_ref, o_ref, xn_ref, acc_ref, *, nj):
    j = pl.program_id(1)

    @pl.when(j == 0)
    def _():
        xn_ref[...] = _rms(h_ref[...], pre_ref[...]).astype(BF16)
        acc_ref[...] = jnp.zeros_like(acc_ref)

    xn = xn_ref[...]
    gate = _dot(xn, wg_ref[...])
    up = _dot(xn, wu_ref[...])
    act = (gate * _sigmoid(gate) * up).astype(BF16)
    acc_ref[...] += _dot(act, wo_ref[...])

    @pl.when(j == nj - 1)
    def _():
        o_ref[...] = h_ref[...] + 0.5 * _rms(acc_ref[...], post_ref[...])


def _ffn_half(h, pre_g, w_in, w_out, post_g, *, tm, tf):
    T, D = h.shape
    F = w_out.shape[0]
    nj = F // tf
    return pl.pallas_call(
        functools.partial(_ffn_kernel, nj=nj),
        grid=(T // tm, nj),
        in_specs=[
            pl.BlockSpec((tm, D), lambda i, j: (i, 0)),
            pl.BlockSpec((1, D), lambda i, j: (0, 0)),
            pl.BlockSpec((D, tf), lambda i, j: (0, j)),
            pl.BlockSpec((D, tf), lambda i, j: (0, j + nj)),
            pl.BlockSpec((tf, D), lambda i, j: (j, 0)),
            pl.BlockSpec((1, D), lambda i, j: (0, 0)),
        ],
        out_specs=pl.BlockSpec((tm, D), lambda i, j: (i, 0)),
        out_shape=jax.ShapeDtypeStruct((T, D), F32),
        scratch_shapes=[pltpu.VMEM((tm, D), BF16), pltpu.VMEM((tm, D), F32)],
        compiler_params=_cparams(("parallel", "arbitrary")),
        name="ffn_half",
    )(h, pre_g, w_in, w_in, w_out, post_g)


def _norm_linear_kernel(x_ref, g_ref, w_ref, *rest, gate):
    if gate:
        b_ref, o_ref, xn_ref = rest
    else:
        o_ref, xn_ref = rest

    @pl.when(pl.program_id(1) == 0)
    def _():
        xn_ref[...] = _rms(x_ref[...], g_ref[...]).astype(BF16)

    y = _dot(xn_ref[...], w_ref[...])
    if gate:
        y = _sigmoid(y + b_ref[...])
    o_ref[...] = y.astype(o_ref.dtype)


def _norm_linear(x, g, w, bias=None, *, tm, tn, name):
    T, D = x.shape
    N = w.shape[1]
    in_specs = [
        pl.BlockSpec((tm, D), lambda i, j: (i, 0)),
        pl.BlockSpec((1, D), lambda i, j: (0, 0)),
        pl.BlockSpec((D, tn), lambda i, j: (0, j)),
    ]
    args = [x, g, w]
    if bias is not None:
        in_specs.append(pl.BlockSpec((1, tn), lambda i, j: (0, j)))
        args.append(bias)
    return pl.pallas_call(
        functools.partial(_norm_linear_kernel, gate=bias is not None),
        grid=(T // tm, N // tn),
        in_specs=in_specs,
        out_specs=pl.BlockSpec((tm, tn), lambda i, j: (i, j)),
        out_shape=jax.ShapeDtypeStruct((T, N), F32),
        scratch_shapes=[pltpu.VMEM((tm, D), BF16)],
        compiler_params=_cparams(("parallel", "arbitrary")),
        name=name,
    )(*args)


def _gelu_tanh(x):
    c = np.sqrt(2.0 / np.pi).astype(np.float32)
    return 0.5 * x * (1.0 + jnp.tanh(c * (x + 0.044715 * (x * x * x))))


def _gmlp_kernel(z_ref, lng_ref, lnb_ref, w_ref, b_ref, um_ref, v_ref):
    a = _gelu_tanh(z_ref[...])
    u = a[:, :A_WIDTH]
    v = a[:, A_WIDTH:]
    mu = jnp.mean(v, axis=-1, keepdims=True)
    vc = v - mu
    var = jnp.mean(vc * vc, axis=-1, keepdims=True)
    vn = vc * lax.rsqrt(var + EPS) * lng_ref[...] + lnb_ref[...]
    v_ref[...] = vn
    vb = vn.astype(BF16)
    mixed = jnp.concatenate(
        [_dot(w_ref[0, g], vb[:, g * A_GROUP_W:(g + 1) * A_GROUP_W]) for g in range(A_GROUPS)], axis=1)
    um_ref[...] = (u * (mixed + b_ref[0])).astype(BF16)


def _gmlp(z_small, ln_g, ln_b, mix_w, mix_b, *, n_prompt_chunks):
    T = z_small.shape[0]
    n_chunks = T // CHUNK
    mode = lambda c: c // n_prompt_chunks
    return pl.pallas_call(
        _gmlp_kernel,
        grid=(n_chunks,),
        in_specs=[
            pl.BlockSpec((CHUNK, 2 * A_WIDTH), lambda c: (c, OFF_UV // (2 * A_WIDTH))),
            pl.BlockSpec((1, A_WIDTH), lambda c: (0, 0)),
            pl.BlockSpec((1, A_WIDTH), lambda c: (0, 0)),
            pl.BlockSpec((1, A_GROUPS, CHUNK, CHUNK), lambda c: (mode(c), 0, 0, 0)),
            pl.BlockSpec((1, CHUNK, A_WIDTH), lambda c: (mode(c), 0, 0)),
        ],
        out_specs=[pl.BlockSpec((CHUNK, A_WIDTH), lambda c: (c, 0)),
                   pl.BlockSpec((CHUNK, A_WIDTH), lambda c: (c, 0))],
        out_shape=[jax.ShapeDtypeStruct((T, A_WIDTH), BF16),
                   jax.ShapeDtypeStruct((T, A_WIDTH), F32)],
        compiler_params=_cparams(("parallel",)),
        name="gmlp_chunk",
    )(z_small, ln_g, ln_b, mix_w, mix_b)


def _mla_prep_kernel(zc_ref, zk_ref, cos_ref, sin_ref, qg_ref, kvg_ref, wuq_ref, wuk_ref,
                     q_ref, ckv_ref, kr_ref, kc_ref):
    zc = zc_ref[...]
    cqn = _rms(zc[:, :MLA_Q_LORA], qg_ref[...]).astype(BF16)
    q = _dot(cqn, wuq_ref[...])
    nope_w = MLA_HEADS * MLA_NOPE
    slot_w = MLA_HEADS * LANES
    cos = cos_ref[...]
    sin = sin_ref[...]
    cos_h = jnp.concatenate([cos] * MLA_HEADS, axis=1)
    sin_h = jnp.concatenate([sin] * MLA_HEADS, axis=1)
    q_rope = q[:, nope_w:nope_w + slot_w] * cos_h + q[:, nope_w + slot_w:] * sin_h
    for h in range(MLA_HEADS):
        q_nope = q[:, h * MLA_NOPE:(h + 1) * MLA_NOPE].astype(BF16)
        q_lat = _dot(q_nope, wuk_ref[h])
        q_ref[:, h * MLA_QW:h * MLA_QW + LANES] = (q_lat * MLA_SCALE).astype(BF16)
        q_ref[:, h * MLA_QW + LANES:(h + 1) * MLA_QW] = (
            q_rope[:, h * LANES:(h + 1) * LANES] * MLA_SCALE).astype(BF16)
    ckv = _rms(zc[:, MLA_Q_LORA:], kvg_ref[...])
    zk = zk_ref[...]
    kr = zk[:, :LANES] * cos + zk[:, LANES:] * sin
    ckv_ref[...] = ckv
    kr_ref[...] = kr
    kc_ref[:, :LANES] = ckv.astype(BF16)
    kc_ref[:, LANES:] = kr.astype(BF16)


def _mla_prep(z_small, cos, sin, q_norm_g, kv_norm_g, wuq, wuk_t, *, tm):
    T = z_small.shape[0]
    cw = MLA_Q_LORA + MLA_KV_LORA
    return pl.pallas_call(
        _mla_prep_kernel,
        grid=(T // tm,),
        in_specs=[
            pl.BlockSpec((tm, cw), lambda i: (i, OFF_CQ // cw)),
            pl.BlockSpec((tm, 2 * LANES), lambda i: (i, OFF_KR // (2 * LANES))),
            pl.BlockSpec((tm, LANES), lambda i: (i, 0)),
            pl.BlockSpec((tm, LANES), lambda i: (i, 0)),
            pl.BlockSpec((1, MLA_Q_LORA), lambda i: (0, 0)),
            pl.BlockSpec((1, MLA_KV_LORA), lambda i: (0, 0)),
            pl.BlockSpec(wuq.shape, lambda i: (0, 0)),
            pl.BlockSpec(wuk_t.shape, lambda i: (0, 0, 0)),
        ],
        out_specs=[
            pl.BlockSpec((tm, MLA_HEADS * MLA_QW), lambda i: (i, 0)),
            pl.BlockSpec((tm, LANES), lambda i: (i, 0)),
            pl.BlockSpec((tm, LANES), lambda i: (i, 0)),
            pl.BlockSpec((tm, 2 * LANES), lambda i: (i, 0)),
        ],
        out_shape=[
            jax.ShapeDtypeStruct((T, MLA_HEADS * MLA_QW), BF16),
            jax.ShapeDtypeStruct((T, LANES), F32),
            jax.ShapeDtypeStruct((T, LANES), F32),
            jax.ShapeDtypeStruct((T, 2 * LANES), BF16),
        ],
        compiler_params=_cparams(("parallel",)),
        name="mla_prep",
    )(z_small, z_small, cos, sin, q_norm_g, kv_norm_g, wuq, wuk_t)


def _topk_lanes(sc, lane_f, k):
    picks = []
    for _ in range(k):
        m = jnp.max(sc, axis=1, keepdims=True)
        idx = jnp.min(jnp.where(sc == m, lane_f, 1e9), axis=1, keepdims=True)
        hit = lane_f == idx
        picks.append((idx, m > 0.5 * PICK_NEG, hit))
        sc = jnp.where(hit, PICK_NEG, sc)
    return picks


def _moba_select_kernel(q_ref, kv_ref, qa_ref, ka_ref, vg_ref, means_ref, *, nb):
    i = pl.program_id(1)
    hd = MOBA_HEAD_DIM
    tq = q_ref.shape[0]

    @pl.when(i == 0)
    def _():
        means_ref[...] = jnp.zeros_like(means_ref)

    q = q_ref[...]
    k = kv_ref[:, :KV_W]
    v = kv_ref[:, KV_W:]
    means = means_ref[0:nb, :]
    blk = lax.broadcasted_iota(jnp.int32, (tq, nb), 1)
    blk_f = blk.astype(F32)
    past = blk < i
    pad = jnp.zeros((tq, LANES - hd - nb), F32)
    pieces = []
    for h in range(MOBA_HEADS):
        g = h // MOBA_GROUP
        qh = q[:, h * hd:(h + 1) * hd]
        sc = _dot_nt(qh, means[:, g * hd:(g + 1) * hd], precision=lax.Precision.HIGHEST)
        sc = jnp.where(past, sc, PICK_NEG)
        chosen = blk == i
        for _, ok, hit in _topk_lanes(sc, blk_f, min(MOBA_TOPK, nb)):
            chosen = chosen | (hit & ok)
        pieces += [qh * MOBA_SCALE, jnp.where(chosen, 0.0, 1.0), pad]
    qa_ref[...] = jnp.concatenate(pieces, axis=1).astype(BF16)
    onehot = jnp.where(blk == i, -MASK_BIG, 0.0)
    for g in range(MOBA_KV_HEADS):
        ka_ref[g] = jnp.concatenate([k[:, g * hd:(g + 1) * hd], onehot, pad], axis=1).astype(BF16)
        vg_ref[g] = v[:, g * hd:(g + 1) * hd].astype(BF16)
    means_ref[pl.ds(i, 1), :] = jnp.sum(k, axis=0, keepdims=True) * (1.0 / MOBA_BLOCK)


def _moba_select(z_small, *, B, S):
    nb = S // MOBA_BLOCK
    Tp = B * S
    tq = MOBA_BLOCK
    assert MOBA_HEAD_DIM + nb <= LANES
    return pl.pallas_call(
        functools.partial(_moba_select_kernel, nb=nb),
        grid=(B, nb),
        in_specs=[
            pl.BlockSpec((tq, Q_W), lambda b, i: (b * nb + i, OFF_Q // Q_W)),
            pl.BlockSpec((tq, 2 * KV_W), lambda b, i: (b * nb + i, OFF_K // (2 * KV_W))),
        ],
        out_specs=[
            pl.BlockSpec((tq, MOBA_HEADS * LANES), lambda b, i: (b * nb + i, 0)),
            pl.BlockSpec((MOBA_KV_HEADS, tq, LANES), lambda b, i: (0, b * nb + i, 0)),
            pl.BlockSpec((MOBA_KV_HEADS, tq, MOBA_HEAD_DIM), lambda b, i: (0, b * nb + i, 0)),
        ],
        out_shape=[
            jax.ShapeDtypeStruct((Tp, MOBA_HEADS * LANES), BF16),
            jax.ShapeDtypeStruct((MOBA_KV_HEADS, Tp, LANES), BF16),
            jax.ShapeDtypeStruct((MOBA_KV_HEADS, Tp, MOBA_HEAD_DIM), BF16),
        ],
        scratch_shapes=[pltpu.VMEM((max(nb, 8), KV_W), F32)],
        compiler_params=_cparams(("arbitrary", "arbitrary")),
        name="moba_select",
    )(z_small, z_small)


def _flash_kernel(*refs, nh, tq, tk, qw, dv, shared_kv):
    if shared_kv:
        q_ref, k_ref, o_ref, qs_ref, m_ref, l_ref, acc_ref = refs
    else:
        q_ref, k_ref, v_ref, o_ref, qs_ref, m_ref, l_ref, acc_ref = refs
    qi = pl.program_id(2)
    kj = pl.program_id(3)
    last = (qi * tq + (tq - 1)) // tk
    rows = nh * tq

    @pl.when(kj == 0)
    def _():
        for h in range(nh):
            qs_ref[h * tq:(h + 1) * tq, :] = q_ref[:, h * qw:(h + 1) * qw]
        m_ref[...] = jnp.full_like(m_ref, -jnp.inf)
        l_ref[...] = jnp.zeros_like(l_ref)
        acc_ref[...] = jnp.zeros_like(acc_ref)

    @pl.when(kj <= last)
    def _():
        k = k_ref[0]
        v = k[:, :dv] if shared_kv else v_ref[0]
        s = _dot_nt(qs_ref[...], k)
        qpos = qi * tq + (lax.broadcasted_iota(jnp.int32, (rows, tk), 0) & (tq - 1))
        kpos = kj * tk + lax.broadcasted_iota(jnp.int32, (rows, tk), 1)
        s = jnp.where(kpos <= qpos, s, NEG)
        m_prev = m_ref[...]
        m_new = jnp.maximum(m_prev, jnp.max(s, axis=1, keepdims=True))
        alpha = jnp.exp(m_prev - m_new)
        p = jnp.exp(s - m_new)
        l_ref[...] = alpha * l_ref[...] + jnp.sum(p, axis=1, keepdims=True)
        acc_ref[...] = alpha * acc_ref[...] + _dot(p.astype(BF16), v)
        m_ref[...] = m_new

    @pl.when(kj == last)
    def _():
        out = acc_ref[...] / l_ref[...]
        o_ref[...] = jnp.concatenate(
            [out[h * tq:(h + 1) * tq] for h in range(nh)], axis=1).astype(o_ref.dtype)


def _flash(q, k, v, *, B, S, nh, tq, tk, qw, dv, name):
    G = k.shape[0]
    nq, nk = S // tq, S // tk
    assert tq & (tq - 1) == 0 and S % tq == 0 and S % tk == 0
    shared = v is None
    kmap = lambda b, g, qi, kj: (g, b * nk + jnp.minimum(kj, (qi * tq + (tq - 1)) // tk), 0)
    in_specs = [pl.BlockSpec((tq, nh * qw), lambda b, g, qi, kj: (b * nq + qi, g)),
                pl.BlockSpec((1, tk, qw), kmap)]
    args = [q, k]
    if not shared:
        in_specs.append(pl.BlockSpec((1, tk, dv), kmap))
        args.append(v)
    return pl.pallas_call(
        functools.partial(_flash_kernel, nh=nh, tq=tq, tk=tk, qw=qw, dv=dv, shared_kv=shared),
        grid=(B, G, nq, nk),
        in_specs=in_specs,
        out_specs=pl.BlockSpec((tq, nh * dv), lambda b, g, qi, kj: (b * nq + qi, g)),
        out_shape=jax.ShapeDtypeStruct((B * S, G * nh * dv), BF16),
        scratch_shapes=[pltpu.VMEM((nh * tq, qw), BF16), pltpu.VMEM((nh * tq, 1), F32),
                        pltpu.VMEM((nh * tq, 1), F32), pltpu.VMEM((nh * tq, dv), F32)],
        compiler_params=_cparams(("parallel", "parallel", "parallel", "arbitrary")),
        name=name,
    )(*args)


def _mla_sample_kernel(pt_ref, q_ref, kn_ref, *refs, pp):
    ck = refs[:pp]
    kr = refs[pp:2 * pp]
    o_ref, m_ref, l_ref, acc_ref = refs[2 * pp:]
    c = pl.program_id(1)
    q = q_ref[0]
    q_lat = q[:, :MLA_KV_LORA]
    q_rope = q[:, LANES:LANES + MLA_ROPE]

    @pl.when(c == 0)
    def _():
        kn = kn_ref[0]
        m_ref[...] = jnp.sum(q.astype(F32) * kn, axis=1, keepdims=True)
        l_ref[...] = jnp.ones_like(l_ref)
        acc_ref[...] = jnp.broadcast_to(kn[:, :MLA_KV_LORA], acc_ref.shape)

    pages = [ck[r][0].astype(BF16) for r in range(pp)]
    s = jnp.concatenate(
        [_dot_nt(q_lat, pages[r]) + _dot(q_rope, kr[r][0].astype(BF16)) for r in range(pp)], axis=1)
    m_prev = m_ref[...]
    m_new = jnp.maximum(m_prev, jnp.max(s, axis=1, keepdims=True))
    alpha = jnp.exp(m_prev - m_new)
    p = jnp.exp(s - m_new)
    l_ref[...] = alpha * l_ref[...] + jnp.sum(p, axis=1, keepdims=True)
    pb = p.astype(BF16)
    pv = _dot(pb[:, :CHUNK], pages[0])
    for r in range(1, pp):
        pv += _dot(pb[:, r * CHUNK:(r + 1) * CHUNK], pages[r])
    acc_ref[...] = alpha * acc_ref[...] + pv
    m_ref[...] = m_new

    @pl.when(c == pl.num_programs(1) - 1)
    def _():
        o_ref[0] = (acc_ref[...] / l_ref[...]).astype(o_ref.dtype)


def _mla_sample(page_table, q_s, kc_new, pool_ckv, pool_kr_t):
    DB, NP = page_table.shape
    page = pool_ckv.shape[1]
    assert page == CHUNK
    pp = SAMPLE_PAGES_PER_STEP
    assert NP % pp == 0
    ck_specs = [pl.BlockSpec((1, page, MLA_KV_LORA), lambda b, c, pt, r=r: (pt[b, c * pp + r], 0, 0))
                for r in range(pp)]
    kr_specs = [pl.BlockSpec((1, MLA_ROPE, page), lambda b, c, pt, r=r: (pt[b, c * pp + r], 0, 0))
                for r in range(pp)]
    grid_spec = pltpu.PrefetchScalarGridSpec(
        num_scalar_prefetch=1,
        grid=(DB, NP // pp),
        in_specs=[pl.BlockSpec((1, MLA_HEADS, MLA_QW), lambda b, c, pt: (b, 0, 0)),
                  pl.BlockSpec((1, 1, MLA_QW), lambda b, c, pt: (b, 0, 0))] + ck_specs + kr_specs,
        out_specs=pl.BlockSpec((1, MLA_HEADS, MLA_KV_LORA), lambda b, c, pt: (b, 0, 0)),
        scratch_shapes=[pltpu.VMEM((MLA_HEADS, 1), F32), pltpu.VMEM((MLA_HEADS, 1), F32),
                        pltpu.VMEM((MLA_HEADS, MLA_KV_LORA), F32)],
    )
    return pl.pallas_call(
        functools.partial(_mla_sample_kernel, pp=pp),
        grid_spec=grid_spec,
        out_shape=jax.ShapeDtypeStruct((DB, MLA_HEADS, MLA_KV_LORA), BF16),
        compiler_params=_cparams(("parallel", "arbitrary")),
        name="mla_sample",
    )(page_table, q_s, kc_new, *([pool_ckv] * pp), *([pool_kr_t] * pp))


def _moba_sample_pick_kernel(pt_ref, q_ref, *refs, pp):
    kp = refs[:pp]
    top_ref, bs_ref = refs[pp:]
    c = pl.program_id(1)
    ppb = MOBA_BLOCK // CHUNK
    nb = bs_ref.shape[1]
    col = lax.broadcasted_iota(jnp.int32, bs_ref.shape, 1)

    @pl.when(c == 0)
    def _():
        bs_ref[...] = jnp.zeros_like(bs_ref)

    sums = bs_ref[...]
    for n in range(pp // ppb):
        tot = kp[n * ppb][0]
        for r in range(1, ppb):
            tot = tot + kp[n * ppb + r][0]
        sums = jnp.where(col == c * (pp // ppb) + n, jnp.sum(tot, axis=1, keepdims=True), sums)
    bs_ref[...] = sums

    @pl.when(c == pl.num_programs(1) - 1)
    def _():
        means_t = sums * (1.0 / MOBA_BLOCK)
        sc = jnp.dot(q_ref[0], means_t, precision=lax.Precision.HIGHEST, preferred_element_type=F32)
        lane_f = lax.broadcasted_iota(jnp.int32, (MOBA_HEADS, nb), 1).astype(F32)
        picks = _topk_lanes(sc, lane_f, MOBA_TOPK)
        out_lane = lax.broadcasted_iota(jnp.int32, (MOBA_HEADS, LANES), 1)
        out = jnp.zeros((MOBA_HEADS, LANES), F32)
        for t, (idx, _, _) in enumerate(picks):
            out = jnp.where(out_lane == t, idx, out)
        top_ref[0] = out.astype(jnp.int32)


def _moba_sample_pick(page_table, q_slot, pool_k_t):
    DB, NP = page_table.shape
    page = pool_k_t.shape[2]
    pp = SAMPLE_PAGES_PER_STEP
    ppb = MOBA_BLOCK // page
    assert page == CHUNK and NP % pp == 0
    nb_past = NP // ppb
    assert nb_past >= MOBA_TOPK
    kp_specs = [pl.BlockSpec((1, KV_W, page), lambda b, c, pt, r=r: (pt[b, c * pp + r], 0, 0))
                for r in range(pp)]
    grid_spec = pltpu.PrefetchScalarGridSpec(
        num_scalar_prefetch=1,
        grid=(DB, NP // pp),
        in_specs=[pl.BlockSpec((1, MOBA_HEADS, KV_W), lambda b, c, pt: (b, 0, 0))] + kp_specs,
        out_specs=pl.BlockSpec((1, MOBA_HEADS, LANES), lambda b, c, pt: (b, 0, 0)),
        scratch_shapes=[pltpu.VMEM((KV_W, nb_past), F32)],
    )
    return pl.pallas_call(
        functools.partial(_moba_sample_pick_kernel, pp=pp),
        grid_spec=grid_spec,
        out_shape=jax.ShapeDtypeStruct((DB, MOBA_HEADS, LANES), jnp.int32),
        compiler_params=_cparams(("parallel", "arbitrary")),
        name="moba_sample_pick",
    )(page_table, q_slot, *([pool_k_t] * pp))


def _moba_sample_attend_kernel(ph_ref, q_ref, kn_ref, vn_ref, *refs, npg):
    kp = refs[:npg]
    vp = refs[npg:2 * npg]
    o_ref = refs[2 * npg]
    q1 = q_ref[0, 0]
    q = jnp.broadcast_to(q1, (8, LANES)).astype(BF16)
    s_own = jnp.sum(q1 * kn_ref[0], axis=1, keepdims=True)
    s = jnp.concatenate([_dot(q, kp[r][0].astype(BF16)) for r in range(npg)], axis=1)
    m = jnp.maximum(jnp.max(s, axis=1, keepdims=True), s_own)
    p = jnp.exp(s - m)
    p_own = jnp.exp(s_own - m)
    l = jnp.sum(p, axis=1, keepdims=True) + p_own
    pb = p.astype(BF16)
    acc = p_own * vn_ref[0]
    for r in range(npg):
        acc += _dot_nt(pb[:, r * CHUNK:(r + 1) * CHUNK], vp[r][0].astype(BF16))
    o_ref[0, 0] = (acc / l)[0:1]


def _moba_sample_attend(phys, q_slot, k_new, v_new, pool_k_t, pool_v_t):
    DB, H = q_slot.shape[:2]
    page = pool_k_t.shape[2]
    npg = MOBA_TOPK * (MOBA_BLOCK // page)
    pmap = lambda b, h, ph, r: (ph[(b * H + h) * npg + r], 0, 0)
    kp_specs = [pl.BlockSpec((1, KV_W, page), functools.partial(pmap, r=r)) for r in range(npg)]
    vp_specs = [pl.BlockSpec((1, KV_W, page), functools.partial(pmap, r=r)) for r in range(npg)]
    grid_spec = pltpu.PrefetchScalarGridSpec(
        num_scalar_prefetch=1,
        grid=(DB, H),
        in_specs=[pl.BlockSpec((1, 1, 1, LANES), lambda b, h, ph: (b, h, 0, 0)),
                  pl.BlockSpec((1, 1, KV_W), lambda b, h, ph: (b, 0, 0)),
                  pl.BlockSpec((1, 1, KV_W), lambda b, h, ph: (b, 0, 0))] + kp_specs + vp_specs,
        out_specs=pl.BlockSpec((1, 1, 1, LANES), lambda b, h, ph: (b, h, 0, 0)),
    )
    return pl.pallas_call(
        functools.partial(_moba_sample_attend_kernel, npg=npg),
        grid_spec=grid_spec,
        out_shape=jax.ShapeDtypeStruct((DB, H, 1, LANES), F32),
        compiler_params=_cparams(("parallel", "parallel")),
        name="moba_sample_attend",
    )(phys, q_slot, k_new, v_new, *([pool_k_t] * npg), *([pool_v_t] * npg))


def _merge_kernel(um_ref, ab_ref, ol_ref, g0_ref, g1_ref, g2_ref, wa_ref, wb_ref, wc_ref, wuv_ref, o_ref):
    y_a = _dot(um_ref[...], wa_ref[...])
    y_b = _dot(ab_ref[...], wb_ref[...])
    ol = ol_ref[...]
    oc = jnp.concatenate(
        [_dot(ol[:, h * MLA_KV_LORA:(h + 1) * MLA_KV_LORA], wuv_ref[h]) for h in range(MLA_HEADS)],
        axis=1).astype(BF16)
    y_c = _dot(oc, wc_ref[...])
    o_ref[...] = (g0_ref[...] * y_a + g1_ref[...] * y_b + g2_ref[...] * y_c).astype(o_ref.dtype)


def _merge(um, attn_b, o_lat, gates, w_a, w_b, w_c, wuv_t, *, tm):
    T = um.shape[0]
    D = w_a.shape[1]
    row = lambda w: pl.BlockSpec((tm, w), lambda i: (i, 0))
    full = lambda a: pl.BlockSpec(a.shape, lambda i: (0,) * a.ndim)
    return pl.pallas_call(
        _merge_kernel,
        grid=(T // tm,),
        in_specs=[row(um.shape[1]), row(attn_b.shape[1]), row(o_lat.shape[1]),
                  pl.BlockSpec((tm, D), lambda i: (i, 0)),
                  pl.BlockSpec((tm, D), lambda i: (i, 1)),
                  pl.BlockSpec((tm, D), lambda i: (i, 2)),
                  full(w_a), full(w_b), full(w_c), full(wuv_t)],
        out_specs=pl.BlockSpec((tm, D), lambda i: (i, 0)),
        out_shape=jax.ShapeDtypeStruct((T, D), BF16),
        compiler_params=_cparams(("parallel",)),
        name="merge",
    )(um, attn_b, o_lat, gates, gates, gates, w_a, w_b, w_c, wuv_t)


def _out_proj_kernel(x_ref, w_ref, h_ref, g_ref, o_ref):
    o_ref[...] = h_ref[...] + _rms(_dot(x_ref[...], w_ref[...]), g_ref[...])


def _out_proj(x, w, h, post_g, *, tm):
    T, D = h.shape
    return pl.pallas_call(
        _out_proj_kernel,
        grid=(T // tm,),
        in_specs=[pl.BlockSpec((tm, x.shape[1]), lambda i: (i, 0)),
                  pl.BlockSpec(w.shape, lambda i: (0, 0)),
                  pl.BlockSpec((tm, D), lambda i: (i, 0)),
                  pl.BlockSpec((1, D), lambda i: (0, 0))],
        out_specs=pl.BlockSpec((tm, D), lambda i: (i, 0)),
        out_shape=jax.ShapeDtypeStruct((T, D), F32),
        compiler_params=_cparams(("parallel",)),
        name="out_proj",
    )(x, w, h, post_g)


def _rot_half_cols(w):
    half = MLA_ROPE // 2
    return jnp.concatenate([-w[..., half:], w[..., :half]], axis=-1)


def _pad_lanes(w):
    return jnp.pad(w, [(0, 0)] * (w.ndim - 1) + [(0, LANES - w.shape[-1])])


def _pack_mix_w_in(w):
    offs = np.cumsum([0, 2 * A_WIDTH, Q_W, KV_W, KV_W, MLA_Q_LORA, MLA_KV_LORA, MLA_ROPE])
    uv, q, k, v, cq, ckv, kr = [w[:, offs[n]:offs[n + 1]] for n in range(7)]
    gate = w[:, offs[7]:]
    small = jnp.concatenate([uv, q, cq, ckv, k, v, _pad_lanes(kr), _pad_lanes(_rot_half_cols(kr))], axis=1)
    assert small.shape[1] == N_SMALL
    return small.astype(BF16), gate.astype(BF16)


def _pack_w_uq(w):
    d = w.shape[0]
    w3 = w.reshape(d, MLA_HEADS, MLA_NOPE + MLA_ROPE)
    nope = w3[:, :, :MLA_NOPE].reshape(d, MLA_HEADS * MLA_NOPE)
    rope = w3[:, :, MLA_NOPE:]
    slot = lambda x: _pad_lanes(x).reshape(d, MLA_HEADS * LANES)
    return jnp.concatenate([nope, slot(rope), slot(_rot_half_cols(rope))], axis=1).astype(BF16)


def _rope_tables(pos):
    half = MLA_ROPE // 2
    inv = ROPE_BASE ** (-jnp.arange(half, dtype=F32) / half)
    ang = pos.astype(F32)[:, None] * inv
    cos, sin = jnp.cos(ang), jnp.sin(ang)
    return (_pad_lanes(jnp.concatenate([cos, cos], axis=1)),
            _pad_lanes(jnp.concatenate([sin, sin], axis=1)))


def kernel(x_prompt, x_sample, cache_moba_k, cache_moba_v, cache_mla_ckv, cache_mla_krope, page_table, ffn1_pre_g, ffn1_w_in, ffn1_w_out, ffn1_post_g, mix_pre_g, mix_w_in, mix_gate_b, sgu_ln_g, sgu_ln_b, sgu_w, sgu_b, sgu_proj, moba_proj, mla_q_norm_g, mla_w_uq, mla_kv_norm_g, mla_w_uk, mla_w_uv, mla_proj, mix_w_out, mix_post_g, ffn2_pre_g, ffn2_w_in, ffn2_w_out, ffn2_post_g):
    B, S, D = x_prompt.shape
    DB, dec_seq, _ = x_sample.shape
    depth = ffn1_w_in.shape[0]
    F = ffn1_w_out.shape[1]
    n_pool, page = cache_moba_k.shape[1:3]
    NP = page_table.shape[1]
    past = NP * page
    Tp = B * S
    T = Tp + DB
    assert dec_seq == 1 and DB % CHUNK == 0 and S % MOBA_BLOCK == 0 and past % MOBA_BLOCK == 0

    tm = _pick_tile(T, 640)
    tf = _pick_tile(F, 512)
    tn_small = _pick_tile(N_SMALL, 512)
    tn_gate = _pick_tile(N_BRANCH * D, 512)
    tm_merge = _pick_tile(T, 320)
    mla_tq = min(128, S)
    mla_tk = min(512, S)

    pos = jnp.concatenate([jnp.tile(jnp.arange(S, dtype=jnp.int32), B),
                           past + jnp.zeros((DB,), jnp.int32)])
    cos, sin = _rope_tables(pos)
    row = lambda a: a.reshape(1, -1)
    eye = jnp.eye(CHUNK, dtype=F32)
    tril = jnp.tril(jnp.ones((CHUNK, CHUNK), dtype=bool))
    pt_flat = page_table.astype(jnp.int32)

    h = jnp.concatenate([x_prompt.reshape(Tp, D), x_sample.reshape(DB, D)], axis=0)
    outs = [[] for _ in range(9)]
    for l in range(depth):
        h = _ffn_half(h, row(ffn1_pre_g[l]), ffn1_w_in[l].astype(BF16), ffn1_w_out[l].astype(BF16),
                      row(ffn1_post_g[l]), tm=tm, tf=tf)

        w_small, w_gate = _pack_mix_w_in(mix_w_in[l])
        pre_g = row(mix_pre_g[l])
        z = _norm_linear(h, pre_g, w_small, tm=tm, tn=tn_small, name="in_proj")
        gates = _norm_linear(h, pre_g, w_gate, row(mix_gate_b[l]), tm=tm, tn=tn_gate, name="gate_proj")

        w_s = sgu_w[l]
        mix_w = jnp.stack([jnp.where(tril, w_s, 0.0), w_s[:, 0, 0][:, None, None] * eye]).astype(BF16)
        b_s = sgu_b[l]
        mix_b = jnp.stack([jnp.repeat(b_s.T, A_GROUP_W, axis=1),
                           jnp.broadcast_to(jnp.repeat(b_s[:, 0], A_GROUP_W)[None], (CHUNK, A_WIDTH))])
        um, v_rows = _gmlp(z, row(sgu_ln_g[l]), row(sgu_ln_b[l]), mix_w, mix_b, n_prompt_chunks=Tp // CHUNK)

        wuk_t = jnp.transpose(mla_w_uk[l], (1, 2, 0)).astype(BF16)
        q_mla, ckv, krope, kc = _mla_prep(z, cos, sin, row(mla_q_norm_g[l]), row(mla_kv_norm_g[l]),
                                          _pack_w_uq(mla_w_uq[l]), wuk_t, tm=tm)

        q_aug, k_aug, v_g = _moba_select(z, B=B, S=S)
        attn_p = _flash(q_aug, k_aug, v_g, B=B, S=S, nh=MOBA_GROUP, tq=MOBA_BLOCK, tk=MOBA_BLOCK,
                        qw=LANES, dv=MOBA_HEAD_DIM, name="moba_prompt")
        olat_p = _flash(q_mla, kc[None], None, B=B, S=S, nh=MLA_HEADS, tq=mla_tq, tk=mla_tk,
                        qw=MLA_QW, dv=MLA_KV_LORA, name="mla_prompt")

        zs = z[Tp:]
        q_s = zs[:, OFF_Q:OFF_Q + Q_W].reshape(DB, MOBA_HEADS, MOBA_HEAD_DIM)
        k_new = zs[:, OFF_K:OFF_K + KV_W]
        v_new = zs[:, OFF_V:OFF_V + KV_W]
        pool_k_t = jnp.transpose(cache_moba_k[l], (0, 2, 3, 1)).reshape(n_pool, KV_W, page)
        pool_v_t = jnp.transpose(cache_moba_v[l], (0, 2, 3, 1)).reshape(n_pool, KV_W, page)
        pool_kr_t = jnp.swapaxes(cache_mla_krope[l], 1, 2)
        kvh = jnp.arange(MOBA_HEADS) // MOBA_GROUP
        lane_kvh = jnp.arange(KV_W) // MOBA_HEAD_DIM
        q_slot = jnp.where(lane_kvh[None, None, :] == kvh[None, :, None],
                           jnp.tile(q_s * MOBA_SCALE, (1, 1, MOBA_KV_HEADS)), 0.0)
        top = _moba_sample_pick(pt_flat, q_slot, pool_k_t)[:, :, :MOBA_TOPK]
        ppb = MOBA_BLOCK // page
        logical = (top[..., None] * ppb + jnp.arange(ppb, dtype=jnp.int32)).reshape(DB, -1)
        phys = jnp.take_along_axis(pt_flat, logical, axis=1).reshape(-1)
        o_slot = _moba_sample_attend(phys, q_slot.reshape(DB, MOBA_HEADS, 1, KV_W),
                                     k_new.reshape(DB, 1, KV_W), v_new.reshape(DB, 1, KV_W),
                                     pool_k_t, pool_v_t)
        o_slot = o_slot.reshape(DB, MOBA_HEADS, MOBA_KV_HEADS, MOBA_HEAD_DIM)
        attn_s = jnp.concatenate([o_slot[:, hh, hh // MOBA_GROUP] for hh in range(MOBA_HEADS)], axis=1)
        kc_new = jnp.concatenate([ckv[Tp:], krope[Tp:]], axis=1).reshape(DB, 1, MLA_QW)
        olat_s = _mla_sample(pt_flat, q_mla[Tp:].reshape(DB, MLA_HEADS, MLA_QW), kc_new,
                             cache_mla_ckv[l], pool_kr_t)

        attn_b = jnp.concatenate([attn_p, attn_s.astype(BF16)], axis=0)
        o_lat = jnp.concatenate([olat_p, olat_s.reshape(DB, MLA_HEADS * MLA_KV_LORA)], axis=0)
        wuv_t = jnp.transpose(mla_w_uv[l], (1, 0, 2)).astype(BF16)
        merged = _merge(um, attn_b, o_lat, gates, sgu_proj[l].astype(BF16), moba_proj[l].astype(BF16),
                        mla_proj[l].astype(BF16), wuv_t, tm=tm_merge)
        h = _out_proj(merged, mix_w_out[l].astype(BF16), h, row(mix_post_g[l]), tm=tm)

        h = _ffn_half(h, row(ffn2_pre_g[l]), ffn2_w_in[l].astype(BF16), ffn2_w_out[l].astype(BF16),
                      row(ffn2_post_g[l]), tm=tm, tf=tf)

        k_all = z[:, OFF_K:OFF_K + KV_W]
        v_all = z[:, OFF_V:OFF_V + KV_W]
        kr_all = krope[:, :MLA_ROPE]
        outs[0].append(k_all[:Tp].reshape(B, S, MOBA_KV_HEADS, MOBA_HEAD_DIM))
        outs[1].append(v_all[:Tp].reshape(B, S, MOBA_KV_HEADS, MOBA_HEAD_DIM))
        outs[2].append(ckv[:Tp].reshape(B, S, MLA_KV_LORA))
        outs[3].append(kr_all[:Tp].reshape(B, S, MLA_ROPE))
        outs[4].append(k_all[Tp:].reshape(DB, 1, MOBA_KV_HEADS, MOBA_HEAD_DIM))
        outs[5].append(v_all[Tp:].reshape(DB, 1, MOBA_KV_HEADS, MOBA_HEAD_DIM))
        outs[6].append(ckv[Tp:].reshape(DB, 1, MLA_KV_LORA))
        outs[7].append(kr_all[Tp:].reshape(DB, 1, MLA_ROPE))
        outs[8].append(v_rows[Tp:].reshape(DB, 1, A_WIDTH))

    return (h[:Tp].reshape(B, S, D), h[Tp:].reshape(DB, 1, D)) + tuple(jnp.stack(o) for o in outs)
```

```python
import functools

import numpy as np
import jax
import jax.numpy as jnp
from jax import lax
from jax.experimental import pallas as pl
from jax.experimental.pallas import tpu as pltpu

F32 = jnp.float32
BF16 = jnp.bfloat16

CHUNK = 128
A_GROUPS = 4
A_GROUP_W = 128
A_WIDTH = A_GROUPS * A_GROUP_W
MOBA_HEADS = 8
MOBA_KV_HEADS = 2
MOBA_GROUP = MOBA_HEADS // MOBA_KV_HEADS
MOBA_HEAD_DIM = 64
MOBA_BLOCK = 256
MOBA_TOPK = 3
MLA_HEADS = 8
MLA_NOPE = 64
MLA_ROPE = 32
MLA_V = 64
MLA_Q_LORA = 384
MLA_KV_LORA = 128
MLA_SCALE = (MLA_NOPE + MLA_ROPE) ** -0.5
MOBA_SCALE = MOBA_HEAD_DIM ** -0.5
ROPE_BASE = 10000.0
N_BRANCH = 3
EPS = 1e-6
NEG = -1e30

LANES = 128
VMEM_LIMIT = 56 * 1024 * 1024
MASK_BIG = 2.0 ** 100
PICK_NEG = -3.0e38

KV_W = MOBA_KV_HEADS * MOBA_HEAD_DIM
Q_W = MOBA_HEADS * MOBA_HEAD_DIM
OFF_UV = 0
OFF_Q = OFF_UV + 2 * A_WIDTH
OFF_CQ = OFF_Q + Q_W
OFF_CKV = OFF_CQ + MLA_Q_LORA
OFF_K = OFF_CKV + MLA_KV_LORA
OFF_V = OFF_K + KV_W
OFF_KR = OFF_V + KV_W
OFF_KRROT = OFF_KR + LANES
N_SMALL = OFF_KRROT + LANES
MLA_QW = 2 * LANES
SAMPLE_PAGES_PER_STEP = 32


def _cparams(sem, vmem=VMEM_LIMIT):
    return pltpu.CompilerParams(dimension_semantics=sem, vmem_limit_bytes=vmem)


def _pick_tile(n, cap, mult=LANES):
    best = None
    t = mult
    while t <= min(n, cap):
        if n % t == 0:
            best = t
        t += mult
    assert best is not None, (n, cap, mult)
    return best


def _rms(x, g):
    ms = jnp.mean(x * x, axis=-1, keepdims=True)
    return x * lax.rsqrt(ms + EPS) * g


def _sigmoid(x):
    return 1.0 / (1.0 + jnp.exp(-x))


def _dot(a, b):
    return jnp.dot(a, b, preferred_element_type=F32)


def _dot_nt(a, b, precision=None):
    return lax.dot_general(a, b, (((1,), (1,)), ((), ())), precision=precision,
                           preferred_element_type=F32)


def _ffn_kernel(h_ref, pre_ref, wg_ref, wu_ref, wo_ref, post_ref, o_ref, xn_ref, acc_ref, *, nj):
    j = pl.program_id(1)

    @pl.when(j == 0)
    def _():
        xn_ref[...] = _rms(h_ref[...], pre_ref[...]).astype(BF16)
        acc_ref[...] = jnp.zeros_like(acc_ref)

    xn = xn_ref[...]
    gate = _dot(xn, wg_ref[...])
    up = _dot(xn, wu_ref[...])
    act = (gate * _sigmoid(gate) * up).astype(BF16)
    acc_ref[...] += _dot(act, wo_ref[...])

    @pl.when(j == nj - 1)
    def _():
        o_ref[...] = h_ref[...] + 0.5 * _rms(acc_ref[...], post_ref[...])


def _ffn_half(h, pre_g, w_in, w_out, post_g, *, tm, tf):
    T, D = h.shape
    F = w_out.shape[0]
    nj = F // tf
    return pl.pallas_call(
        functools.partial(_ffn_kernel, nj=nj),
        grid=(T // tm, nj),
        in_specs=[
            pl.BlockSpec((tm, D), lambda i, j: (i, 0)),
            pl.BlockSpec((1, D), lambda i, j: (0, 0)),
            pl.BlockSpec((D, tf), lambda i, j: (0, j)),
            pl.BlockSpec((D, tf), lambda i, j: (0, j + nj)),
            pl.BlockSpec((tf, D), lambda i, j: (j, 0)),
            pl.BlockSpec((1, D), lambda i, j: (0, 0)),
        ],
        out_specs=pl.BlockSpec((tm, D), lambda i, j: (i, 0)),
        out_shape=jax.ShapeDtypeStruct((T, D), F32),
        scratch_shapes=[pltpu.VMEM((tm, D), BF16), pltpu.VMEM((tm, D), F32)],
        compiler_params=_cparams(("parallel", "arbitrary")),
        name="ffn_half",
    )(h, pre_g, w_in, w_in, w_out, post_g)


def _norm_linear_kernel(x_ref, g_ref, w_ref, *rest, gate):
    if gate:
        b_ref, o_ref, xn_ref = rest
    else:
        o_ref, xn_ref = rest

    @pl.when(pl.program_id(1) == 0)
    def _():
        xn_ref[...] = _rms(x_ref[...], g_ref[...]).astype(BF16)

    y = _dot(xn_ref[...], w_ref[...])
    if gate:
        y = _sigmoid(y + b_ref[...])
    o_ref[...] = y.astype(o_ref.dtype)


def _norm_linear(x, g, w, bias=None, *, tm, tn, name):
    T, D = x.shape
    N = w.shape[1]
    in_specs = [
        pl.BlockSpec((tm, D), lambda i, j: (i, 0)),
        pl.BlockSpec((1, D), lambda i, j: (0, 0)),
        pl.BlockSpec((D, tn), lambda i, j: (0, j)),
    ]
    args = [x, g, w]
    if bias is not None:
        in_specs.append(pl.BlockSpec((1, tn), lambda i, j: (0, j)))
        args.append(bias)
    return pl.pallas_call(
        functools.partial(_norm_linear_kernel, gate=bias is not None),
        grid=(T // tm, N // tn),
        in_specs=in_specs,
        out_specs=pl.BlockSpec((tm, tn), lambda i, j: (i, j)),
        out_shape=jax.ShapeDtypeStruct((T, N), F32),
        scratch_shapes=[pltpu.VMEM((tm, D), BF16)],
        compiler_params=_cparams(("parallel", "arbitrary")),
        name=name,
    )(*args)


def _gelu_tanh(x):
    c = np.sqrt(2.0 / np.pi).astype(np.float32)
    return 0.5 * x * (1.0 + jnp.tanh(c * (x + 0.044715 * (x * x * x))))


def _gmlp_kernel(z_ref, lng_ref, lnb_ref, w_ref, b_ref, um_ref, v_ref):
    a = _gelu_tanh(z_ref[...])
    u = a[:, :A_WIDTH]
    v = a[:, A_WIDTH:]
    mu = jnp.mean(v, axis=-1, keepdims=True)
    vc = v - mu
    var = jnp.mean(vc * vc, axis=-1, keepdims=True)
    vn = vc * lax.rsqrt(var + EPS) * lng_ref[...] + lnb_ref[...]
    v_ref[...] = vn
    vb = vn.astype(BF16)
    mixed = jnp.concatenate(
        [_dot(w_ref[0, g], vb[:, g * A_GROUP_W:(g + 1) * A_GROUP_W]) for g in range(A_GROUPS)], axis=1)
    um_ref[...] = (u * (mixed + b_ref[0])).astype(BF16)


def _gmlp(z_small, ln_g, ln_b, mix_w, mix_b, *, n_prompt_chunks):
    T = z_small.shape[0]
    n_chunks = T // CHUNK
    mode = lambda c: c // n_prompt_chunks
    return pl.pallas_call(
        _gmlp_kernel,
        grid=(n_chunks,),
        in_specs=[
            pl.BlockSpec((CHUNK, 2 * A_WIDTH), lambda c: (c, OFF_UV // (2 * A_WIDTH))),
            pl.BlockSpec((1, A_WIDTH), lambda c: (0, 0)),
            pl.BlockSpec((1, A_WIDTH), lambda c: (0, 0)),
            pl.BlockSpec((1, A_GROUPS, CHUNK, CHUNK), lambda c: (mode(c), 0, 0, 0)),
            pl.BlockSpec((1, CHUNK, A_WIDTH), lambda c: (mode(c), 0, 0)),
        ],
        out_specs=[pl.BlockSpec((CHUNK, A_WIDTH), lambda c: (c, 0)),
                   pl.BlockSpec((CHUNK, A_WIDTH), lambda c: (c, 0))],
        out_shape=[jax.ShapeDtypeStruct((T, A_WIDTH), BF16),
                   jax.ShapeDtypeStruct((T, A_WIDTH), F32)],
        compiler_params=_cparams(("parallel",)),
        name="gmlp_chunk",
    )(z_small, ln_g, ln_b, mix_w, mix_b)


def _mla_prep_kernel(zc_ref, zk_ref, cos_ref, sin_ref, qg_ref, kvg_ref, wuq_ref, wuk_ref,
                     q_ref, ckv_ref, kr_ref, kc_ref):
    zc = zc_ref[...]
    cqn = _rms(zc[:, :MLA_Q_LORA], qg_ref[...]).astype(BF16)
    q = _dot(cqn, wuq_ref[...])
    nope_w = MLA_HEADS * MLA_NOPE
    slot_w = MLA_HEADS * LANES
    cos = cos_ref[...]
    sin = sin_ref[...]
    cos_h = jnp.concatenate([cos] * MLA_HEADS, axis=1)
    sin_h = jnp.concatenate([sin] * MLA_HEADS, axis=1)
    q_rope = q[:, nope_w:nope_w + slot_w] * cos_h + q[:, nope_w + slot_w:] * sin_h
    for h in range(MLA_HEADS):
        q_nope = q[:, h * MLA_NOPE:(h + 1) * MLA_NOPE].astype(BF16)
        q_lat = _dot(q_nope, wuk_ref[h])
        q_ref[:, h * MLA_QW:h * MLA_QW + LANES] = (q_lat * MLA_SCALE).astype(BF16)
        q_ref[:, h * MLA_QW + LANES:(h + 1) * MLA_QW] = (
            q_rope[:, h * LANES:(h + 1) * LANES] * MLA_SCALE).astype(BF16)
    ckv = _rms(zc[:, MLA_Q_LORA:], kvg_ref[...])
    zk = zk_ref[...]
    kr = zk[:, :LANES] * cos + zk[:, LANES:] * sin
    ckv_ref[...] = ckv
    kr_ref[...] = kr
    kc_ref[:, :LANES] = ckv.astype(BF16)
    kc_ref[:, LANES:] = kr.astype(BF16)


def _mla_prep(z_small, cos, sin, q_norm_g, kv_norm_g, wuq, wuk_t, *, tm):
    T = z_small.shape[0]
    cw = MLA_Q_LORA + MLA_KV_LORA
    return pl.pallas_call(
        _mla_prep_kernel,
        grid=(T // tm,),
        in_specs=[
            pl.BlockSpec((tm, cw), lambda i: (i, OFF_CQ // cw)),
            pl.BlockSpec((tm, 2 * LANES), lambda i: (i, OFF_KR // (2 * LANES))),
            pl.BlockSpec((tm, LANES), lambda i: (i, 0)),
            pl.BlockSpec((tm, LANES), lambda i: (i, 0)),
            pl.BlockSpec((1, MLA_Q_LORA), lambda i: (0, 0)),
            pl.BlockSpec((1, MLA_KV_LORA), lambda i: (0, 0)),
            pl.BlockSpec(wuq.shape, lambda i: (0, 0)),
            pl.BlockSpec(wuk_t.shape, lambda i: (0, 0, 0)),
        ],
        out_specs=[
            pl.BlockSpec((tm, MLA_HEADS * MLA_QW), lambda i: (i, 0)),
            pl.BlockSpec((tm, LANES), lambda i: (i, 0)),
            pl.BlockSpec((tm, LANES), lambda i: (i, 0)),
            pl.BlockSpec((tm, 2 * LANES), lambda i: (i, 0)),
        ],
        out_shape=[
            jax.ShapeDtypeStruct((T, MLA_HEADS * MLA_QW), BF16),
            jax.ShapeDtypeStruct((T, LANES), F32),
            jax.ShapeDtypeStruct((T, LANES), F32),
            jax.ShapeDtypeStruct((T, 2 * LANES), BF16),
        ],
        compiler_params=_cparams(("parallel",)),
        name="mla_prep",
    )(z_small, z_small, cos, sin, q_norm_g, kv_norm_g, wuq, wuk_t)


def _topk_lanes(sc, lane_f, k):
    picks = []
    for _ in range(k):
        m = jnp.max(sc, axis=1, keepdims=True)
        idx = jnp.min(jnp.where(sc == m, lane_f, 1e9), axis=1, keepdims=True)
        hit = lane_f == idx
        picks.append((idx, m > 0.5 * PICK_NEG, hit))
        sc = jnp.where(hit, PICK_NEG, sc)
    return picks


def _moba_select_kernel(q_ref, k_ref, qa_ref, ka_ref, means_ref, *, nb):
    i = pl.program_id(1)
    hd = MOBA_HEAD_DIM
    tq = q_ref.shape[0]

    @pl.when(i == 0)
    def _():
        means_ref[...] = jnp.zeros_like(means_ref)

    q = q_ref[...]
    k = k_ref[...]
    means = means_ref[0:nb, :]
    blk = lax.broadcasted_iota(jnp.int32, (tq, nb), 1)
    blk_f = blk.astype(F32)
    past = blk < i
    pad = jnp.zeros((tq, LANES - hd - nb), F32)
    pieces = []
    for h in range(MOBA_HEADS):
        g = h // MOBA_GROUP
        qh = q[:, h * hd:(h + 1) * hd]
        sc = _dot_nt(qh, means[:, g * hd:(g + 1) * hd], precision=lax.Precision.HIGHEST)
        sc = jnp.where(past, sc, PICK_NEG)
        chosen = blk == i
        for _, ok, hit in _topk_lanes(sc, blk_f, min(MOBA_TOPK, nb)):
            chosen = chosen | (hit & ok)
        pieces += [qh * MOBA_SCALE, jnp.where(chosen, 0.0, 1.0), pad]
    qa_ref[...] = jnp.concatenate(pieces, axis=1).astype(BF16)
    onehot = jnp.where(blk == i, -MASK_BIG, 0.0)
    for g in range(MOBA_KV_HEADS):
        ka_ref[g] = jnp.concatenate([k[:, g * hd:(g + 1) * hd], onehot, pad], axis=1).astype(BF16)
    means_ref[pl.ds(i, 1), :] = jnp.sum(k, axis=0, keepdims=True) * (1.0 / MOBA_BLOCK)


def _moba_select(z_small, *, B, S):
    nb = S // MOBA_BLOCK
    Tp = B * S
    tq = MOBA_BLOCK
    assert MOBA_HEAD_DIM + nb <= LANES
    return pl.pallas_call(
        functools.partial(_moba_select_kernel, nb=nb),
        grid=(B, nb),
        in_specs=[
            pl.BlockSpec((tq, Q_W), lambda b, i: (b * nb + i, OFF_Q // Q_W)),
            pl.BlockSpec((tq, KV_W), lambda b, i: (b * nb + i, OFF_K // KV_W)),
        ],
        out_specs=[
            pl.BlockSpec((tq, MOBA_HEADS * LANES), lambda b, i: (b * nb + i, 0)),
            pl.BlockSpec((MOBA_KV_HEADS, tq, LANES), lambda b, i: (0, b * nb + i, 0)),
        ],
        out_shape=[
            jax.ShapeDtypeStruct((Tp, MOBA_HEADS * LANES), BF16),
            jax.ShapeDtypeStruct((MOBA_KV_HEADS, Tp, LANES), BF16),
        ],
        scratch_shapes=[pltpu.VMEM((max(nb, 8), KV_W), F32)],
        compiler_params=_cparams(("arbitrary", "arbitrary")),
        name="moba_select",
    )(z_small, z_small)


def _flash_kernel(q_ref, k_ref, vt_ref, o_ref, qt_ref, m_ref, l_ref, acc_ref, *, nh, tq, tk, qw, dv):
    qi = pl.program_id(2)
    kj = pl.program_id(3)
    last = (qi * tq + (tq - 1)) // tk
    cols = nh * tq

    @pl.when(kj == 0)
    def _():
        for h in range(nh):
            qt_ref[:, h * tq:(h + 1) * tq] = q_ref[:, h * qw:(h + 1) * qw].astype(F32).T.astype(BF16)
        m_ref[...] = jnp.full_like(m_ref, -jnp.inf)
        l_ref[...] = jnp.zeros_like(l_ref)
        acc_ref[...] = jnp.zeros_like(acc_ref)

    def step(masked):
        s = _dot(k_ref[0], qt_ref[...])
        if masked:
            kpos = kj * tk + lax.broadcasted_iota(jnp.int32, (tk, cols), 0)
            qpos = qi * tq + (lax.broadcasted_iota(jnp.int32, (tk, cols), 1) & (tq - 1))
            s = jnp.where(kpos <= qpos, s, NEG)
        m_prev = m_ref[...]
        m_new = jnp.maximum(m_prev, jnp.max(s, axis=0, keepdims=True))
        alpha = jnp.exp(m_prev - m_new)
        p = jnp.exp(s - m_new)
        l_ref[...] = alpha * l_ref[...] + jnp.sum(p, axis=0, keepdims=True)
        acc_ref[...] = alpha * acc_ref[...] + _dot(vt_ref[0], p.astype(BF16))
        m_ref[...] = m_new

    @pl.when(kj < last)
    def _():
        step(False)

    @pl.when(kj == last)
    def _():
        step(True)
        out_t = acc_ref[...] / l_ref[...]
        stacked = jnp.concatenate([out_t[:, h * tq:(h + 1) * tq] for h in range(nh)], axis=0)
        o_ref[...] = stacked.T.astype(o_ref.dtype)


def _flash(q, k, vt, *, B, S, nh, tq, tk, qw, dv, name):
    G = k.shape[0]
    nq, nk = S // tq, S // tk
    assert tq & (tq - 1) == 0 and S % tq == 0 and S % tk == 0
    last_k = lambda b, qi, kj: b * nk + jnp.minimum(kj, (qi * tq + (tq - 1)) // tk)
    return pl.pallas_call(
        functools.partial(_flash_kernel, nh=nh, tq=tq, tk=tk, qw=qw, dv=dv),
        grid=(B, G, nq, nk),
        in_specs=[pl.BlockSpec((tq, nh * qw), lambda b, g, qi, kj: (b * nq + qi, g)),
                  pl.BlockSpec((1, tk, qw), lambda b, g, qi, kj: (g, last_k(b, qi, kj), 0)),
                  pl.BlockSpec((1, dv, tk), lambda b, g, qi, kj: (g, 0, last_k(b, qi, kj)))],
        out_specs=pl.BlockSpec((tq, nh * dv), lambda b, g, qi, kj: (b * nq + qi, g)),
        out_shape=jax.ShapeDtypeStruct((B * S, G * nh * dv), BF16),
        scratch_shapes=[pltpu.VMEM((qw, nh * tq), BF16), pltpu.VMEM((1, nh * tq), F32),
                        pltpu.VMEM((1, nh * tq), F32), pltpu.VMEM((dv, nh * tq), F32)],
        compiler_params=_cparams(("parallel", "parallel", "parallel", "arbitrary")),
        name=name,
    )(q, k, vt)


def _mla_sample_kernel(pt_ref, q_ref, kn_ref, *refs, pp):
    ck = refs[:pp]
    kr = refs[pp:2 * pp]
    o_ref, m_ref, l_ref, acc_ref = refs[2 * pp:]
    c = pl.program_id(1)
    q = q_ref[0]
    q_lat = q[:, :MLA_KV_LORA]
    q_rope = q[:, LANES:LANES + MLA_ROPE]

    @pl.when(c == 0)
    def _():
        kn = kn_ref[0]
        m_ref[...] = jnp.sum(q.astype(F32) * kn, axis=1, keepdims=True)
        l_ref[...] = jnp.ones_like(l_ref)
        acc_ref[...] = jnp.broadcast_to(kn[:, :MLA_KV_LORA], acc_ref.shape)

    ckv = jnp.concatenate([ck[r][0, 0].astype(BF16) for r in range(pp)], axis=0)
    kr_t = jnp.concatenate([kr[r][0, 0].astype(BF16) for r in range(pp)], axis=1)
    s = _dot_nt(q_lat, ckv) + _dot(q_rope, kr_t)
    m_prev = m_ref[...]
    m_new = jnp.maximum(m_prev, jnp.max(s, axis=1, keepdims=True))
    alpha = jnp.exp(m_prev - m_new)
    p = jnp.exp(s - m_new)
    l_ref[...] = alpha * l_ref[...] + jnp.sum(p, axis=1, keepdims=True)
    acc_ref[...] = alpha * acc_ref[...] + _dot(p.astype(BF16), ckv)
    m_ref[...] = m_new

    @pl.when(c == pl.num_programs(1) - 1)
    def _():
        o_ref[0] = (acc_ref[...] / l_ref[...]).astype(o_ref.dtype)


def _mla_sample(page_table, q_s, kc_new, cache_ckv, cache_kr_t, layer):
    DB, NP = page_table.shape
    page = cache_ckv.shape[2]
    assert page == CHUNK
    pp = SAMPLE_PAGES_PER_STEP
    assert NP % pp == 0
    pmap = lambda b, c, pt, r: (layer, pt[b, c * pp + r], 0, 0)
    ck_specs = [pl.BlockSpec((1, 1, page, MLA_KV_LORA), functools.partial(pmap, r=r)) for r in range(pp)]
    kr_specs = [pl.BlockSpec((1, 1, MLA_ROPE, page), functools.partial(pmap, r=r)) for r in range(pp)]
    grid_spec = pltpu.PrefetchScalarGridSpec(
        num_scalar_prefetch=1,
        grid=(DB, NP // pp),
        in_specs=[pl.BlockSpec((1, MLA_HEADS, MLA_QW), lambda b, c, pt: (b, 0, 0)),
                  pl.BlockSpec((1, 1, MLA_QW), lambda b, c, pt: (b, 0, 0))] + ck_specs + kr_specs,
        out_specs=pl.BlockSpec((1, MLA_HEADS, MLA_KV_LORA), lambda b, c, pt: (b, 0, 0)),
        scratch_shapes=[pltpu.VMEM((MLA_HEADS, 1), F32), pltpu.VMEM((MLA_HEADS, 1), F32),
                        pltpu.VMEM((MLA_HEADS, MLA_KV_LORA), F32)],
    )
    return pl.pallas_call(
        functools.partial(_mla_sample_kernel, pp=pp),
        grid_spec=grid_spec,
        out_shape=jax.ShapeDtypeStruct((DB, MLA_HEADS, MLA_KV_LORA), BF16),
        compiler_params=_cparams(("parallel", "arbitrary")),
        name="mla_sample",
    )(page_table, q_s, kc_new, *([cache_ckv] * pp), *([cache_kr_t] * pp))


def _moba_sample_pick_kernel(pt_ref, q_ref, *refs, pp):
    kp = refs[:pp]
    top_ref, bs_ref = refs[pp:]
    c = pl.program_id(1)
    ppb = MOBA_BLOCK // CHUNK
    nb = bs_ref.shape[1]
    col = lax.broadcasted_iota(jnp.int32, bs_ref.shape, 1)

    @pl.when(c == 0)
    def _():
        bs_ref[...] = jnp.zeros_like(bs_ref)

    sums = bs_ref[...]
    for n in range(pp // ppb):
        tot = kp[n * ppb][0, 0]
        for r in range(1, ppb):
            tot = tot + kp[n * ppb + r][0, 0]
        sums = jnp.where(col == c * (pp // ppb) + n, jnp.sum(tot, axis=1, keepdims=True), sums)
    bs_ref[...] = sums

    @pl.when(c == pl.num_programs(1) - 1)
    def _():
        means_t = sums * (1.0 / MOBA_BLOCK)
        sc = jnp.dot(q_ref[0], means_t, precision=lax.Precision.HIGHEST, preferred_element_type=F32)
        lane_f = lax.broadcasted_iota(jnp.int32, (MOBA_HEADS, nb), 1).astype(F32)
        picks = _topk_lanes(sc, lane_f, MOBA_TOPK)
        out_lane = lax.broadcasted_iota(jnp.int32, (MOBA_HEADS, LANES), 1)
        out = jnp.zeros((MOBA_HEADS, LANES), F32)
        for t, (idx, _, _) in enumerate(picks):
            out = jnp.where(out_lane == t, idx, out)
        top_ref[0] = out.astype(jnp.int32)


def _moba_sample_pick(page_table, q_slot, cache_k_t, layer):
    DB, NP = page_table.shape
    page = cache_k_t.shape[3]
    pp = SAMPLE_PAGES_PER_STEP
    ppb = MOBA_BLOCK // page
    assert page == CHUNK and NP % pp == 0
    nb_past = NP // ppb
    assert nb_past >= MOBA_TOPK
    kp_specs = [pl.BlockSpec((1, 1, KV_W, page), lambda b, c, pt, r=r: (layer, pt[b, c * pp + r], 0, 0))
                for r in range(pp)]
    grid_spec = pltpu.PrefetchScalarGridSpec(
        num_scalar_prefetch=1,
        grid=(DB, NP // pp),
        in_specs=[pl.BlockSpec((1, MOBA_HEADS, KV_W), lambda b, c, pt: (b, 0, 0))] + kp_specs,
        out_specs=pl.BlockSpec((1, MOBA_HEADS, LANES), lambda b, c, pt: (b, 0, 0)),
        scratch_shapes=[pltpu.VMEM((KV_W, nb_past), F32)],
    )
    return pl.pallas_call(
        functools.partial(_moba_sample_pick_kernel, pp=pp),
        grid_spec=grid_spec,
        out_shape=jax.ShapeDtypeStruct((DB, MOBA_HEADS, LANES), jnp.int32),
        compiler_params=_cparams(("parallel", "arbitrary")),
        name="moba_sample_pick",
    )(page_table, q_slot, *([cache_k_t] * pp))


def _moba_sample_attend_kernel(ph_ref, q_ref, kn_ref, vn_ref, *refs, npg):
    nh = MOBA_HEADS
    kp = refs[:nh * npg]
    vp = refs[nh * npg:2 * nh * npg]
    o_ref = refs[2 * nh * npg]
    q = q_ref[0]
    qb = q.astype(BF16)
    row = lax.broadcasted_iota(jnp.int32, (nh, npg * CHUNK), 0)
    s = jnp.zeros((nh, npg * CHUNK), F32)
    for h in range(nh):
        k_t = jnp.concatenate([kp[h * npg + r][0, 0].astype(BF16) for r in range(npg)], axis=1)
        s = jnp.where(row == h, _dot(qb, k_t), s)
    s_own = jnp.sum(q * kn_ref[0], axis=1, keepdims=True)
    m = jnp.maximum(jnp.max(s, axis=1, keepdims=True), s_own)
    p = jnp.exp(s - m)
    p_own = jnp.exp(s_own - m)
    l = jnp.sum(p, axis=1, keepdims=True) + p_own
    pb = p.astype(BF16)
    acc = p_own * vn_ref[0]
    orow = lax.broadcasted_iota(jnp.int32, (nh, KV_W), 0)
    for h in range(nh):
        v_t = jnp.concatenate([vp[h * npg + r][0, 0].astype(BF16) for r in range(npg)], axis=1)
        acc = acc + jnp.where(orow == h, _dot_nt(pb, v_t), 0.0)
    o_ref[0] = acc / l


def _moba_sample_attend(phys, q_slot, k_new, v_new, cache_k_t, cache_v_t, layer):
    DB, H = q_slot.shape[:2]
    page = cache_k_t.shape[3]
    npg = MOBA_TOPK * (MOBA_BLOCK // page)
    pmap = lambda b, ph, n: (layer, ph[b * (H * npg) + n], 0, 0)
    kp_specs = [pl.BlockSpec((1, 1, KV_W, page), functools.partial(pmap, n=n)) for n in range(H * npg)]
    vp_specs = [pl.BlockSpec((1, 1, KV_W, page), functools.partial(pmap, n=n)) for n in range(H * npg)]
    grid_spec = pltpu.PrefetchScalarGridSpec(
        num_scalar_prefetch=1,
        grid=(DB,),
        in_specs=[pl.BlockSpec((1, H, KV_W), lambda b, ph: (b, 0, 0)),
                  pl.BlockSpec((1, 1, KV_W), lambda b, ph: (b, 0, 0)),
                  pl.BlockSpec((1, 1, KV_W), lambda b, ph: (b, 0, 0))] + kp_specs + vp_specs,
        out_specs=pl.BlockSpec((1, H, KV_W), lambda b, ph: (b, 0, 0)),
    )
    return pl.pallas_call(
        functools.partial(_moba_sample_attend_kernel, npg=npg),
        grid_spec=grid_spec,
        out_shape=jax.ShapeDtypeStruct((DB, H, KV_W), F32),
        compiler_params=_cparams(("parallel",)),
        name="moba_sample_attend",
    )(phys, q_slot, k_new, v_new, *([cache_k_t] * (H * npg)), *([cache_v_t] * (H * npg)))


def _merge_kernel(um_ref, ab_ref, ol_ref, g0_ref, g1_ref, g2_ref, wa_ref, wb_ref, wc_ref, wuv_ref, o_ref):
    y_a = _dot(um_ref[...], wa_ref[...])
    y_b = _dot(ab_ref[...], wb_ref[...])
    ol = ol_ref[...]
    oc = jnp.concatenate(
        [_dot(ol[:, h * MLA_KV_LORA:(h + 1) * MLA_KV_LORA], wuv_ref[h]) for h in range(MLA_HEADS)],
        axis=1).astype(BF16)
    y_c = _dot(oc, wc_ref[...])
    o_ref[...] = (g0_ref[...] * y_a + g1_ref[...] * y_b + g2_ref[...] * y_c).astype(o_ref.dtype)


def _merge(um, attn_b, o_lat, gates, w_a, w_b, w_c, wuv_t, *, tm):
    T = um.shape[0]
    D = w_a.shape[1]
    row = lambda w: pl.BlockSpec((tm, w), lambda i: (i, 0))
    full = lambda a: pl.BlockSpec(a.shape, lambda i: (0,) * a.ndim)
    return pl.pallas_call(
        _merge_kernel,
        grid=(T // tm,),
        in_specs=[row(um.shape[1]), row(attn_b.shape[1]), row(o_lat.shape[1]),
                  pl.BlockSpec((tm, D), lambda i: (i, 0)),
                  pl.BlockSpec((tm, D), lambda i: (i, 1)),
                  pl.BlockSpec((tm, D), lambda i: (i, 2)),
                  full(w_a), full(w_b), full(w_c), full(wuv_t)],
        out_specs=pl.BlockSpec((tm, D), lambda i: (i, 0)),
        out_shape=jax.ShapeDtypeStruct((T, D), BF16),
        compiler_params=_cparams(("parallel",)),
        name="merge",
    )(um, attn_b, o_lat, gates, gates, gates, w_a, w_b, w_c, wuv_t)


def _out_proj_kernel(x_ref, w_ref, h_ref, g_ref, o_ref):
    o_ref[...] = h_ref[...] + _rms(_dot(x_ref[...], w_ref[...]), g_ref[...])


def _out_proj(x, w, h, post_g, *, tm):
    T, D = h.shape
    return pl.pallas_call(
        _out_proj_kernel,
        grid=(T // tm,),
        in_specs=[pl.BlockSpec((tm, x.shape[1]), lambda i: (i, 0)),
                  pl.BlockSpec(w.shape, lambda i: (0, 0)),
                  pl.BlockSpec((tm, D), lambda i: (i, 0)),
                  pl.BlockSpec((1, D), lambda i: (0, 0))],
        out_specs=pl.BlockSpec((tm, D), lambda i: (i, 0)),
        out_shape=jax.ShapeDtypeStruct((T, D), F32),
        compiler_params=_cparams(("parallel",)),
        name="out_proj",
    )(x, w, h, post_g)


def _rot_half_cols(w):
    half = MLA_ROPE // 2
    return jnp.concatenate([-w[..., half:], w[..., :half]], axis=-1)


def _pad_lanes(w):
    return jnp.pad(w, [(0, 0)] * (w.ndim - 1) + [(0, LANES - w.shape[-1])])


def _pack_mix_w_in(w):
    offs = np.cumsum([0, 2 * A_WIDTH, Q_W, KV_W, KV_W, MLA_Q_LORA, MLA_KV_LORA, MLA_ROPE])
    uv, q, k, v, cq, ckv, kr = [w[:, offs[n]:offs[n + 1]] for n in range(7)]
    gate = w[:, offs[7]:]
    small = jnp.concatenate([uv, q, cq, ckv, k, v, _pad_lanes(kr), _pad_lanes(_rot_half_cols(kr))], axis=1)
    assert small.shape[1] == N_SMALL
    return small.astype(BF16), gate.astype(BF16)


def _pack_w_uq(w):
    d = w.shape[0]
    w3 = w.reshape(d, MLA_HEADS, MLA_NOPE + MLA_ROPE)
    nope = w3[:, :, :MLA_NOPE].reshape(d, MLA_HEADS * MLA_NOPE)
    rope = w3[:, :, MLA_NOPE:]
    slot = lambda x: _pad_lanes(x).reshape(d, MLA_HEADS * LANES)
    return jnp.concatenate([nope, slot(rope), slot(_rot_half_cols(rope))], axis=1).astype(BF16)


def _rope_tables(pos):
    half = MLA_ROPE // 2
    inv = ROPE_BASE ** (-jnp.arange(half, dtype=F32) / half)
    ang = pos.astype(F32)[:, None] * inv
    cos, sin = jnp.cos(ang), jnp.sin(ang)
    return (_pad_lanes(jnp.concatenate([cos, cos], axis=1)),
            _pad_lanes(jnp.concatenate([sin, sin], axis=1)))


def kernel(x_prompt, x_sample, cache_moba_k, cache_moba_v, cache_mla_ckv, cache_mla_krope, page_table, ffn1_pre_g, ffn1_w_in, ffn1_w_out, ffn1_post_g, mix_pre_g, mix_w_in, mix_gate_b, sgu_ln_g, sgu_ln_b, sgu_w, sgu_b, sgu_proj, moba_proj, mla_q_norm_g, mla_w_uq, mla_kv_norm_g, mla_w_uk, mla_w_uv, mla_proj, mix_w_out, mix_post_g, ffn2_pre_g, ffn2_w_in, ffn2_w_out, ffn2_post_g):
    B, S, D = x_prompt.shape
    DB, dec_seq, _ = x_sample.shape
    depth = ffn1_w_in.shape[0]
    F = ffn1_w_out.shape[1]
    n_pool, page = cache_moba_k.shape[1:3]
    NP = page_table.shape[1]
    past = NP * page
    Tp = B * S
    T = Tp + DB
    assert dec_seq == 1 and DB % CHUNK == 0 and S % MOBA_BLOCK == 0 and past % MOBA_BLOCK == 0

    tm = _pick_tile(T, 640)
    tf = _pick_tile(F, 512)
    tn_small = _pick_tile(N_SMALL, 512)
    tn_gate = _pick_tile(N_BRANCH * D, 512)
    tm_merge = _pick_tile(T, 320)
    mla_tq = min(128, S)
    mla_tk = _pick_tile(S, 512)
    moba_tk = _pick_tile(S, 512, MOBA_BLOCK)

    cache_k_t = jnp.transpose(cache_moba_k, (0, 1, 3, 4, 2)).reshape(depth, n_pool, KV_W, page)
    cache_v_t = jnp.transpose(cache_moba_v, (0, 1, 3, 4, 2)).reshape(depth, n_pool, KV_W, page)
    cache_kr_t = jnp.swapaxes(cache_mla_krope, 2, 3)

    pos = jnp.concatenate([jnp.tile(jnp.arange(S, dtype=jnp.int32), B),
                           past + jnp.zeros((DB,), jnp.int32)])
    cos, sin = _rope_tables(pos)
    row = lambda a: a.reshape(1, -1)
    eye = jnp.eye(CHUNK, dtype=F32)
    tril = jnp.tril(jnp.ones((CHUNK, CHUNK), dtype=bool))
    pt_flat = page_table.astype(jnp.int32)

    h = jnp.concatenate([x_prompt.reshape(Tp, D), x_sample.reshape(DB, D)], axis=0)
    outs = [[] for _ in range(9)]
    for l in range(depth):
        h = _ffn_half(h, row(ffn1_pre_g[l]), ffn1_w_in[l].astype(BF16), ffn1_w_out[l].astype(BF16),
                      row(ffn1_post_g[l]), tm=tm, tf=tf)

        w_small, w_gate = _pack_mix_w_in(mix_w_in[l])
        pre_g = row(mix_pre_g[l])
        z = _norm_linear(h, pre_g, w_small, tm=tm, tn=tn_small, name="in_proj")
        gates = _norm_linear(h, pre_g, w_gate, row(mix_gate_b[l]), tm=tm, tn=tn_gate, name="gate_proj")

        w_s = sgu_w[l]
        mix_w = jnp.stack([jnp.where(tril, w_s, 0.0), w_s[:, 0, 0][:, None, None] * eye]).astype(BF16)
        b_s = sgu_b[l]
        mix_b = jnp.stack([jnp.repeat(b_s.T, A_GROUP_W, axis=1),
                           jnp.broadcast_to(jnp.repeat(b_s[:, 0], A_GROUP_W)[None], (CHUNK, A_WIDTH))])
        um, v_rows = _gmlp(z, row(sgu_ln_g[l]), row(sgu_ln_b[l]), mix_w, mix_b, n_prompt_chunks=Tp // CHUNK)

        wuk_t = jnp.transpose(mla_w_uk[l], (1, 2, 0)).astype(BF16)
        q_mla, ckv, krope, kc = _mla_prep(z, cos, sin, row(mla_q_norm_g[l]), row(mla_kv_norm_g[l]),
                                          _pack_w_uq(mla_w_uq[l]), wuk_t, tm=tm)

        q_aug, k_aug = _moba_select(z, B=B, S=S)
        v_t = z[:Tp, OFF_V:OFF_V + KV_W].T.reshape(MOBA_KV_HEADS, MOBA_HEAD_DIM, Tp).astype(BF16)
        attn_p = _flash(q_aug, k_aug, v_t, B=B, S=S, nh=MOBA_GROUP, tq=MOBA_BLOCK, tk=moba_tk,
                        qw=LANES, dv=MOBA_HEAD_DIM, name="moba_prompt")
        ckv_t = kc[:Tp, :MLA_KV_LORA].T[None]
        olat_p = _flash(q_mla, kc[None], ckv_t, B=B, S=S, nh=MLA_HEADS, tq=mla_tq, tk=mla_tk,
                        qw=MLA_QW, dv=MLA_KV_LORA, name="mla_prompt")

        zs = z[Tp:]
        q_s = zs[:, OFF_Q:OFF_Q + Q_W].reshape(DB, MOBA_HEADS, MOBA_HEAD_DIM)
        k_new = zs[:, OFF_K:OFF_K + KV_W]
        v_new = zs[:, OFF_V:OFF_V + KV_W]
        kvh = jnp.arange(MOBA_HEADS) // MOBA_GROUP
        lane_kvh = jnp.arange(KV_W) // MOBA_HEAD_DIM
        q_slot = jnp.where(lane_kvh[None, None, :] == kvh[None, :, None],
                           jnp.tile(q_s * MOBA_SCALE, (1, 1, MOBA_KV_HEADS)), 0.0)
        top = _moba_sample_pick(pt_flat, q_slot, cache_k_t, l)[:, :, :MOBA_TOPK]
        ppb = MOBA_BLOCK // page
        logical = (top[..., None] * ppb + jnp.arange(ppb, dtype=jnp.int32)).reshape(DB, -1)
        phys = jnp.take_along_axis(pt_flat, logical, axis=1).reshape(-1)
        o_slot = _moba_sample_attend(phys, q_slot, k_new.reshape(DB, 1, KV_W), v_new.reshape(DB, 1, KV_W),
                                     cache_k_t, cache_v_t, l)
        o_slot = o_slot.reshape(DB, MOBA_HEADS, MOBA_KV_HEADS, MOBA_HEAD_DIM)
        attn_s = jnp.concatenate([o_slot[:, hh, hh // MOBA_GROUP] for hh in range(MOBA_HEADS)], axis=1)
        kc_new = jnp.concatenate([ckv[Tp:], krope[Tp:]], axis=1).reshape(DB, 1, MLA_QW)
        olat_s = _mla_sample(pt_flat, q_mla[Tp:].reshape(DB, MLA_HEADS, MLA_QW), kc_new,
                             cache_mla_ckv, cache_kr_t, l)

        attn_b = jnp.concatenate([attn_p, attn_s.astype(BF16)], axis=0)
        o_lat = jnp.concatenate([olat_p, olat_s.reshape(DB, MLA_HEADS * MLA_KV_LORA)], axis=0)
        wuv_t = jnp.transpose(mla_w_uv[l], (1, 0, 2)).astype(BF16)
        merged = _merge(um, attn_b, o_lat, gates, sgu_proj[l].astype(BF16), moba_proj[l].astype(BF16),
                        mla_proj[l].astype(BF16), wuv_t, tm=tm_merge)
        h = _out_proj(merged, mix_w_out[l].astype(BF16), h, row(mix_post_g[l]), tm=tm)

        h = _ffn_half(h, row(ffn2_pre_g[l]), ffn2_w_in[l].astype(BF16), ffn2_w_out[l].astype(BF16),
                      row(ffn2_post_g[l]), tm=tm, tf=tf)

        k_all = z[:, OFF_K:OFF_K + KV_W]
        v_all = z[:, OFF_V:OFF_V + KV_W]
        kr_all = krope[:, :MLA_ROPE]
        outs[0].append(k_all[:Tp].reshape(B, S, MOBA_KV_HEADS, MOBA_HEAD_DIM))
        outs[1].append(v_all[:Tp].reshape(B, S, MOBA_KV_HEADS, MOBA_HEAD_DIM))
        outs[2].append(ckv[:Tp].reshape(B, S, MLA_KV_LORA))
        outs[3].append(kr_all[:Tp].reshape(B, S, MLA_ROPE))
        outs[4].append(k_all[Tp:].reshape(DB, 1, MOBA_KV_HEADS, MOBA_HEAD_DIM))
        outs[5].append(v_all[Tp:].reshape(DB, 1, MOBA_KV_HEADS, MOBA_HEAD_DIM))
        outs[6].append(ckv[Tp:].reshape(DB, 1, MLA_KV_LORA))
        outs[7].append(kr_all[Tp:].reshape(DB, 1, MLA_ROPE))
        outs[8].append(v_rows[Tp:].reshape(DB, 1, A_WIDTH))

    return (h[:Tp].reshape(B, S, D), h[Tp:].reshape(DB, 1, D)) + tuple(jnp.stack(o) for o in outs)
```

```python
import functools

import numpy as np
import jax
import jax.numpy as jnp
from jax import lax
from jax.experimental import pallas as pl
from jax.experimental.pallas import tpu as pltpu

F32 = jnp.float32
BF16 = jnp.bfloat16

CHUNK = 128
A_GROUPS = 4
A_GROUP_W = 128
A_WIDTH = A_GROUPS * A_GROUP_W
MOBA_HEADS = 8
MOBA_KV_HEADS = 2
MOBA_GROUP = MOBA_HEADS // MOBA_KV_HEADS
MOBA_HEAD_DIM = 64
MOBA_BLOCK = 256
MOBA_TOPK = 3
MLA_HEADS = 8
MLA_NOPE = 64
MLA_ROPE = 32
MLA_V = 64
MLA_Q_LORA = 384
MLA_KV_LORA = 128
MLA_SCALE = (MLA_NOPE + MLA_ROPE) ** -0.5
MOBA_SCALE = MOBA_HEAD_DIM ** -0.5
ROPE_BASE = 10000.0
N_BRANCH = 3
EPS = 1e-6
NEG = -1e30

LANES = 128
VMEM_LIMIT = 56 * 1024 * 1024
MASK_BIG = 2.0 ** 100
PICK_NEG = -3.0e38

KV_W = MOBA_KV_HEADS * MOBA_HEAD_DIM
Q_W = MOBA_HEADS * MOBA_HEAD_DIM
OFF_UV = 0
OFF_Q = OFF_UV + 2 * A_WIDTH
OFF_CQ = OFF_Q + Q_W
OFF_CKV = OFF_CQ + MLA_Q_LORA
OFF_K = OFF_CKV + MLA_KV_LORA
OFF_V = OFF_K + KV_W
OFF_KR = OFF_V + KV_W
OFF_KRROT = OFF_KR + LANES
N_SMALL = OFF_KRROT + LANES
MLA_QW = 2 * LANES
SAMPLE_PAGES_PER_STEP = 32


def _cparams(sem, vmem=VMEM_LIMIT):
    return pltpu.CompilerParams(dimension_semantics=sem, vmem_limit_bytes=vmem)


def _pick_tile(n, cap, mult=LANES):
    best = None
    t = mult
    while t <= min(n, cap):
        if n % t == 0:
            best = t
        t += mult
    assert best is not None, (n, cap, mult)
    return best


def _rms(x, g):
    ms = jnp.mean(x * x, axis=-1, keepdims=True)
    return x * lax.rsqrt(ms + EPS) * g


def _sigmoid(x):
    return 1.0 / (1.0 + jnp.exp(-x))


def _dot(a, b):
    return jnp.dot(a, b, preferred_element_type=F32)


def _dot_nt(a, b, precision=None):
    return lax.dot_general(a, b, (((1,), (1,)), ((), ())), precision=precision,
                           preferred_element_type=F32)


def _ffn_kernel(h_ref, pre_ref, wg_ref, wu_ref, wo_ref, post_ref, o_ref, xn_ref, acc_ref, *, nj):
    j = pl.program_id(1)

    @pl.when(j == 0)
    def _():
        xn_ref[...] = _rms(h_ref[...], pre_ref[...]).astype(BF16)
        acc_ref[...] = jnp.zeros_like(acc_ref)

    xn = xn_ref[...]
    gate = _dot(xn, wg_ref[...])
    up = _dot(xn, wu_ref[...])
    act = (gate * _sigmoid(gate) * up).astype(BF16)
    acc_ref[...] += _dot(act, wo_ref[...])

    @pl.when(j == nj - 1)
    def _():
        o_ref[...] = h_ref[...] + 0.5 * _rms(acc_ref[...], post_ref[...])


def _ffn_half(h, pre_g, w_in, w_out, post_g, *, tm, tf):
    T, D = h.shape
    F = w_out.shape[0]
    nj = F // tf
    return pl.pallas_call(
        functools.partial(_ffn_kernel, nj=nj),
        grid=(T // tm, nj),
        in_specs=[
            pl.BlockSpec((tm, D), lambda i, j: (i, 0)),
            pl.BlockSpec((1, D), lambda i, j: (0, 0)),
            pl.BlockSpec((D, tf), lambda i, j: (0, j)),
            pl.BlockSpec((D, tf), lambda i, j: (0, j + nj)),
            pl.BlockSpec((tf, D), lambda i, j: (j, 0)),
            pl.BlockSpec((1, D), lambda i, j: (0, 0)),
        ],
        out_specs=pl.BlockSpec((tm, D), lambda i, j: (i, 0)),
        out_shape=jax.ShapeDtypeStruct((T, D), F32),
        scratch_shapes=[pltpu.VMEM((tm, D), BF16), pltpu.VMEM((tm, D), F32)],
        compiler_params=_cparams(("parallel", "arbitrary")),
        name="ffn_half",
    )(h, pre_g, w_in, w_in, w_out, post_g)


def _norm_linear_kernel(x_ref, g_ref, w_ref, *rest, gate):
    if gate:
        b_ref, o_ref, xn_ref = rest
    else:
        o_ref, xn_ref = rest

    @pl.when(pl.program_id(1) == 0)
    def _():
        xn_ref[...] = _rms(x_ref[...], g_ref[...]).astype(BF16)

    y = _dot(xn_ref[...], w_ref[...])
    if gate:
        y = _sigmoid(y + b_ref[...])
    o_ref[...] = y.astype(o_ref.dtype)


def _norm_linear(x, g, w, bias=None, *, tm, tn, name):
    T, D = x.shape
    N = w.shape[1]
    in_specs = [
        pl.BlockSpec((tm, D), lambda i, j: (i, 0)),
        pl.BlockSpec((1, D), lambda i, j: (0, 0)),
        pl.BlockSpec((D, tn), lambda i, j: (0, j)),
    ]
    args = [x, g, w]
    if bias is not None:
        in_specs.append(pl.BlockSpec((1, tn), lambda i, j: (0, j)))
        args.append(bias)
    return pl.pallas_call(
        functools.partial(_norm_linear_kernel, gate=bias is not None),
        grid=(T // tm, N // tn),
        in_specs=in_specs,
        out_specs=pl.BlockSpec((tm, tn), lambda i, j: (i, j)),
        out_shape=jax.ShapeDtypeStruct((T, N), F32),
        scratch_shapes=[pltpu.VMEM((tm, D), BF16)],
        compiler_params=_cparams(("parallel", "arbitrary")),
        name=name,
    )(*args)


def _gelu_tanh(x):
    c = np.sqrt(2.0 / np.pi).astype(np.float32)
    return 0.5 * x * (1.0 + jnp.tanh(c * (x + 0.044715 * (x * x * x))))


def _gmlp_kernel(z_ref, lng_ref, lnb_ref, w_ref, b_ref, um_ref, v_ref):
    a = _gelu_tanh(z_ref[...])
    u = a[:, :A_WIDTH]
    v = a[:, A_WIDTH:]
    mu = jnp.mean(v, axis=-1, keepdims=True)
    vc = v - mu
    var = jnp.mean(vc * vc, axis=-1, keepdims=True)
    vn = vc * lax.rsqrt(var + EPS) * lng_ref[...] + lnb_ref[...]
    v_ref[...] = vn
    vb = vn.astype(BF16)
    mixed = jnp.concatenate(
        [_dot(w_ref[0, g], vb[:, g * A_GROUP_W:(g + 1) * A_GROUP_W]) for g in range(A_GROUPS)], axis=1)
    um_ref[...] = (u * (mixed + b_ref[0])).astype(BF16)


def _gmlp(z_small, ln_g, ln_b, mix_w, mix_b, *, n_prompt_chunks):
    T = z_small.shape[0]
    n_chunks = T // CHUNK
    mode = lambda c: c // n_prompt_chunks
    return pl.pallas_call(
        _gmlp_kernel,
        grid=(n_chunks,),
        in_specs=[
            pl.BlockSpec((CHUNK, 2 * A_WIDTH), lambda c: (c, OFF_UV // (2 * A_WIDTH))),
            pl.BlockSpec((1, A_WIDTH), lambda c: (0, 0)),
            pl.BlockSpec((1, A_WIDTH), lambda c: (0, 0)),
            pl.BlockSpec((1, A_GROUPS, CHUNK, CHUNK), lambda c: (mode(c), 0, 0, 0)),
            pl.BlockSpec((1, CHUNK, A_WIDTH), lambda c: (mode(c), 0, 0)),
        ],
        out_specs=[pl.BlockSpec((CHUNK, A_WIDTH), lambda c: (c, 0)),
                   pl.BlockSpec((CHUNK, A_WIDTH), lambda c: (c, 0))],
        out_shape=[jax.ShapeDtypeStruct((T, A_WIDTH), BF16),
                   jax.ShapeDtypeStruct((T, A_WIDTH), F32)],
        compiler_params=_cparams(("parallel",)),
        name="gmlp_chunk",
    )(z_small, ln_g, ln_b, mix_w, mix_b)


def _mla_prep_kernel(zc_ref, zk_ref, cos_ref, sin_ref, qg_ref, kvg_ref, wuq_ref, wuk_ref,
                     q_ref, ckv_ref, kr_ref, kc_ref):
    zc = zc_ref[...]
    cqn = _rms(zc[:, :MLA_Q_LORA], qg_ref[...]).astype(BF16)
    q = _dot(cqn, wuq_ref[...])
    nope_w = MLA_HEADS * MLA_NOPE
    slot_w = MLA_HEADS * LANES
    cos = cos_ref[...]
    sin = sin_ref[...]
    cos_h = jnp.concatenate([cos] * MLA_HEADS, axis=1)
    sin_h = jnp.concatenate([sin] * MLA_HEADS, axis=1)
    q_rope = q[:, nope_w:nope_w + slot_w] * cos_h + q[:, nope_w + slot_w:] * sin_h
    for h in range(MLA_HEADS):
        q_nope = q[:, h * MLA_NOPE:(h + 1) * MLA_NOPE].astype(BF16)
        q_lat = _dot(q_nope, wuk_ref[h])
        q_ref[:, h * MLA_QW:h * MLA_QW + LANES] = (q_lat * MLA_SCALE).astype(BF16)
        q_ref[:, h * MLA_QW + LANES:(h + 1) * MLA_QW] = (
            q_rope[:, h * LANES:(h + 1) * LANES] * MLA_SCALE).astype(BF16)
    ckv = _rms(zc[:, MLA_Q_LORA:], kvg_ref[...])
    zk = zk_ref[...]
    kr = zk[:, :LANES] * cos + zk[:, LANES:] * sin
    ckv_ref[...] = ckv
    kr_ref[...] = kr
    kc_ref[:, :LANES] = ckv.astype(BF16)
    kc_ref[:, LANES:] = kr.astype(BF16)


def _mla_prep(z_small, cos, sin, q_norm_g, kv_norm_g, wuq, wuk_t, *, tm):
    T = z_small.shape[0]
    cw = MLA_Q_LORA + MLA_KV_LORA
    return pl.pallas_call(
        _mla_prep_kernel,
        grid=(T // tm,),
        in_specs=[
            pl.BlockSpec((tm, cw), lambda i: (i, OFF_CQ // cw)),
            pl.BlockSpec((tm, 2 * LANES), lambda i: (i, OFF_KR // (2 * LANES))),
            pl.BlockSpec((tm, LANES), lambda i: (i, 0)),
            pl.BlockSpec((tm, LANES), lambda i: (i, 0)),
            pl.BlockSpec((1, MLA_Q_LORA), lambda i: (0, 0)),
            pl.BlockSpec((1, MLA_KV_LORA), lambda i: (0, 0)),
            pl.BlockSpec(wuq.shape, lambda i: (0, 0)),
            pl.BlockSpec(wuk_t.shape, lambda i: (0, 0, 0)),
        ],
        out_specs=[
            pl.BlockSpec((tm, MLA_HEADS * MLA_QW), lambda i: (i, 0)),
            pl.BlockSpec((tm, LANES), lambda i: (i, 0)),
            pl.BlockSpec((tm, LANES), lambda i: (i, 0)),
            pl.BlockSpec((tm, 2 * LANES), lambda i: (i, 0)),
        ],
        out_shape=[
            jax.ShapeDtypeStruct((T, MLA_HEADS * MLA_QW), BF16),
            jax.ShapeDtypeStruct((T, LANES), F32),
            jax.ShapeDtypeStruct((T, LANES), F32),
            jax.ShapeDtypeStruct((T, 2 * LANES), BF16),
        ],
        compiler_params=_cparams(("parallel",)),
        name="mla_prep",
    )(z_small, z_small, cos, sin, q_norm_g, kv_norm_g, wuq, wuk_t)


def _topk_lanes(sc, lane_f, k):
    picks = []
    for _ in range(k):
        m = jnp.max(sc, axis=1, keepdims=True)
        idx = jnp.min(jnp.where(sc == m, lane_f, 1e9), axis=1, keepdims=True)
        hit = lane_f == idx
        picks.append((idx, m > 0.5 * PICK_NEG, hit))
        sc = jnp.where(hit, PICK_NEG, sc)
    return picks


def _moba_select_kernel(q_ref, k_ref, qa_ref, ka_ref, means_ref, *, nb):
    i = pl.program_id(1)
    hd = MOBA_HEAD_DIM
    tq = q_ref.shape[0]

    @pl.when(i == 0)
    def _():
        means_ref[...] = jnp.zeros_like(means_ref)

    q = q_ref[...]
    k = k_ref[...]
    means = means_ref[0:nb, :]
    blk = lax.broadcasted_iota(jnp.int32, (tq, nb), 1)
    blk_f = blk.astype(F32)
    past = blk < i
    pad = jnp.zeros((tq, LANES - hd - nb), F32)
    pieces = []
    for h in range(MOBA_HEADS):
        g = h // MOBA_GROUP
        qh = q[:, h * hd:(h + 1) * hd]
        sc = _dot_nt(qh, means[:, g * hd:(g + 1) * hd], precision=lax.Precision.HIGHEST)
        sc = jnp.where(past, sc, PICK_NEG)
        chosen = blk == i
        for _, ok, hit in _topk_lanes(sc, blk_f, min(MOBA_TOPK, nb)):
            chosen = chosen | (hit & ok)
        pieces += [qh * MOBA_SCALE, jnp.where(chosen, 0.0, 1.0), pad]
    qa_ref[...] = jnp.concatenate(pieces, axis=1).astype(BF16)
    onehot = jnp.where(blk == i, -MASK_BIG, 0.0)
    for g in range(MOBA_KV_HEADS):
        ka_ref[g] = jnp.concatenate([k[:, g * hd:(g + 1) * hd], onehot, pad], axis=1).astype(BF16)
    means_ref[pl.ds(i, 1), :] = jnp.sum(k, axis=0, keepdims=True) * (1.0 / MOBA_BLOCK)


def _moba_select(z_small, *, B, S):
    nb = S // MOBA_BLOCK
    Tp = B * S
    tq = MOBA_BLOCK
    assert MOBA_HEAD_DIM + nb <= LANES
    return pl.pallas_call(
        functools.partial(_moba_select_kernel, nb=nb),
        grid=(B, nb),
        in_specs=[
            pl.BlockSpec((tq, Q_W), lambda b, i: (b * nb + i, OFF_Q // Q_W)),
            pl.BlockSpec((tq, KV_W), lambda b, i: (b * nb + i, OFF_K // KV_W)),
        ],
        out_specs=[
            pl.BlockSpec((tq, MOBA_HEADS * LANES), lambda b, i: (b * nb + i, 0)),
            pl.BlockSpec((MOBA_KV_HEADS, tq, LANES), lambda b, i: (0, b * nb + i, 0)),
        ],
        out_shape=[
            jax.ShapeDtypeStruct((Tp, MOBA_HEADS * LANES), BF16),
            jax.ShapeDtypeStruct((MOBA_KV_HEADS, Tp, LANES), BF16),
        ],
        scratch_shapes=[pltpu.VMEM((max(nb, 8), KV_W), F32)],
        compiler_params=_cparams(("arbitrary", "arbitrary")),
        name="moba_select",
    )(z_small, z_small)


def _flash_kernel(q_ref, k_ref, vt_ref, o_ref, qt_ref, m_ref, l_ref, acc_ref, *, nh, tq, tk, qw, dv):
    qi = pl.program_id(2)
    kj = pl.program_id(3)
    last = (qi * tq + (tq - 1)) // tk
    cols = nh * tq

    @pl.when(kj == 0)
    def _():
        for h in range(nh):
            qt_ref[:, h * tq:(h + 1) * tq] = q_ref[:, h * qw:(h + 1) * qw].astype(F32).T.astype(BF16)
        m_ref[...] = jnp.full_like(m_ref, -jnp.inf)
        l_ref[...] = jnp.zeros_like(l_ref)
        acc_ref[...] = jnp.zeros_like(acc_ref)

    def step(masked):
        s = _dot(k_ref[0], qt_ref[...])
        if masked:
            kpos = kj * tk + lax.broadcasted_iota(jnp.int32, (tk, cols), 0)
            qpos = qi * tq + (lax.broadcasted_iota(jnp.int32, (tk, cols), 1) & (tq - 1))
            s = jnp.where(kpos <= qpos, s, NEG)
        m_prev = m_ref[...]
        m_new = jnp.maximum(m_prev, jnp.max(s, axis=0, keepdims=True))
        alpha = jnp.exp(m_prev - m_new)
        p = jnp.exp(s - m_new)
        l_ref[...] = alpha * l_ref[...] + jnp.sum(p, axis=0, keepdims=True)
        acc_ref[...] = alpha * acc_ref[...] + _dot(vt_ref[0], p.astype(BF16))
        m_ref[...] = m_new

    @pl.when(kj < last)
    def _():
        step(False)

    @pl.when(kj == last)
    def _():
        step(True)
        out_t = acc_ref[...] / l_ref[...]
        stacked = jnp.concatenate([out_t[:, h * tq:(h + 1) * tq] for h in range(nh)], axis=0)
        o_ref[...] = stacked.T.astype(o_ref.dtype)


def _flash(q, k, vt, *, B, S, nh, tq, tk, qw, dv, name):
    G = k.shape[0]
    nq, nk = S // tq, S // tk
    assert tq & (tq - 1) == 0 and S % tq == 0 and S % tk == 0
    last_k = lambda b, qi, kj: b * nk + jnp.minimum(kj, (qi * tq + (tq - 1)) // tk)
    return pl.pallas_call(
        functools.partial(_flash_kernel, nh=nh, tq=tq, tk=tk, qw=qw, dv=dv),
        grid=(B, G, nq, nk),
        in_specs=[pl.BlockSpec((tq, nh * qw), lambda b, g, qi, kj: (b * nq + qi, g)),
                  pl.BlockSpec((1, tk, qw), lambda b, g, qi, kj: (g, last_k(b, qi, kj), 0)),
                  pl.BlockSpec((1, dv, tk), lambda b, g, qi, kj: (g, 0, last_k(b, qi, kj)))],
        out_specs=pl.BlockSpec((tq, nh * dv), lambda b, g, qi, kj: (b * nq + qi, g)),
        out_shape=jax.ShapeDtypeStruct((B * S, G * nh * dv), BF16),
        scratch_shapes=[pltpu.VMEM((qw, nh * tq), BF16), pltpu.VMEM((1, nh * tq), F32),
                        pltpu.VMEM((1, nh * tq), F32), pltpu.VMEM((dv, nh * tq), F32)],
        compiler_params=_cparams(("parallel", "parallel", "parallel", "arbitrary")),
        name=name,
    )(q, k, vt)


def _paged_chunks(page_id, pp, nc, page_copies, compute, carry):
    assert nc % 2 == 0
    seq = pl.program_id(0)
    n_seq = pl.num_programs(0)

    def copies(sq, c):
        out = []
        for r in range(pp):
            out += page_copies(page_id(sq, c * pp + r), r, c % 2)
        return out

    @pl.when(seq == 0)
    def _():
        for cp in copies(seq, 0):
            cp.start()

    for c in range(nc):
        if c + 1 < nc:
            for cp in copies(seq, c + 1):
                cp.start()
        else:
            @pl.when(seq + 1 < n_seq)
            def _():
                for cp in copies(seq + 1, 0):
                    cp.start()
        for cp in copies(seq, c):
            cp.wait()
        carry = compute(c % 2, c, carry)
    return carry


def _mla_sample_kernel(pt_ref, q_ref, kn_ref, ckv_hbm, kr_hbm, o_ref, ckv_buf, kr_buf, sem, *, layer, pp, nc):
    page = CHUNK

    def page_copies(pg, r, slot):
        return [pltpu.make_async_copy(ckv_hbm.at[layer, pg], ckv_buf.at[slot, pl.ds(r * page, page), :],
                                      sem.at[slot, 0]),
                pltpu.make_async_copy(kr_hbm.at[layer, pg], kr_buf.at[slot, :, pl.ds(r * page, page)],
                                      sem.at[slot, 1])]

    q = q_ref[0]
    q_lat = q[:, :MLA_KV_LORA]
    q_rope = q[:, LANES:LANES + MLA_ROPE]
    kn = kn_ref[0]

    def compute(slot, c, carry):
        m_prev, l_prev, acc_prev = carry
        ckv = ckv_buf[slot].astype(BF16)
        kr_t = kr_buf[slot].astype(BF16)
        s = _dot_nt(q_lat, ckv) + _dot(q_rope, kr_t)
        m_new = jnp.maximum(m_prev, jnp.max(s, axis=1, keepdims=True))
        alpha = jnp.exp(m_prev - m_new)
        p = jnp.exp(s - m_new)
        l_new = alpha * l_prev + jnp.sum(p, axis=1, keepdims=True)
        return m_new, l_new, alpha * acc_prev + _dot(p.astype(BF16), ckv)

    init = (jnp.sum(q.astype(F32) * kn, axis=1, keepdims=True),
            jnp.ones((MLA_HEADS, 1), F32),
            jnp.broadcast_to(kn[:, :MLA_KV_LORA], (MLA_HEADS, MLA_KV_LORA)))
    _, l, acc = _paged_chunks(lambda sq, n: pt_ref[sq, n], pp, nc, page_copies, compute, init)
    o_ref[0] = (acc / l).astype(o_ref.dtype)


def _mla_sample(page_table, q_s, kc_new, cache_ckv, cache_kr_t, layer):
    DB, NP = page_table.shape
    page = cache_ckv.shape[2]
    assert page == CHUNK
    pp = SAMPLE_PAGES_PER_STEP
    assert NP % (2 * pp) == 0
    grid_spec = pltpu.PrefetchScalarGridSpec(
        num_scalar_prefetch=1,
        grid=(DB,),
        in_specs=[pl.BlockSpec((1, MLA_HEADS, MLA_QW), lambda b, pt: (b, 0, 0)),
                  pl.BlockSpec((1, 1, MLA_QW), lambda b, pt: (b, 0, 0)),
                  pl.BlockSpec(memory_space=pl.ANY),
                  pl.BlockSpec(memory_space=pl.ANY)],
        out_specs=pl.BlockSpec((1, MLA_HEADS, MLA_KV_LORA), lambda b, pt: (b, 0, 0)),
        scratch_shapes=[pltpu.VMEM((2, pp * page, MLA_KV_LORA), F32),
                        pltpu.VMEM((2, MLA_ROPE, pp * page), F32),
                        pltpu.SemaphoreType.DMA((2, 2))],
    )
    return pl.pallas_call(
        functools.partial(_mla_sample_kernel, layer=layer, pp=pp, nc=NP // pp),
        grid_spec=grid_spec,
        out_shape=jax.ShapeDtypeStruct((DB, MLA_HEADS, MLA_KV_LORA), BF16),
        compiler_params=_cparams(("arbitrary",)),
        name="mla_sample",
    )(page_table, q_s, kc_new, cache_ckv, cache_kr_t)


def _moba_sample_pick_kernel(pt_ref, q_ref, k_hbm, top_ref, k_buf, sem, *, layer, pp, nc):
    ppb = MOBA_BLOCK // CHUNK
    bpc = pp // ppb
    nb = nc * bpc
    col = lax.broadcasted_iota(jnp.int32, (KV_W, nb), 1)

    def page_copies(pg, r, slot):
        return [pltpu.make_async_copy(k_hbm.at[layer, pg], k_buf.at[slot, r], sem.at[slot])]

    def compute(slot, c, sums):
        for n in range(bpc):
            tot = k_buf[slot, n * ppb]
            for r in range(1, ppb):
                tot = tot + k_buf[slot, n * ppb + r]
            sums = jnp.where(col == c * bpc + n, jnp.sum(tot, axis=1, keepdims=True), sums)
        return sums

    sums = _paged_chunks(lambda sq, n: pt_ref[sq, n], pp, nc, page_copies, compute,
                         jnp.zeros((KV_W, nb), F32))
    means_t = sums * (1.0 / MOBA_BLOCK)
    sc = jnp.dot(q_ref[0], means_t, precision=lax.Precision.HIGHEST, preferred_element_type=F32)
    lane_f = lax.broadcasted_iota(jnp.int32, (MOBA_HEADS, nb), 1).astype(F32)
    picks = _topk_lanes(sc, lane_f, MOBA_TOPK)
    out_lane = lax.broadcasted_iota(jnp.int32, (MOBA_HEADS, LANES), 1)
    out = jnp.zeros((MOBA_HEADS, LANES), F32)
    for t, (idx, _, _) in enumerate(picks):
        out = jnp.where(out_lane == t, idx, out)
    top_ref[0] = out.astype(jnp.int32)


def _moba_sample_pick(page_table, q_slot, cache_k_t, layer):
    DB, NP = page_table.shape
    page = cache_k_t.shape[3]
    pp = SAMPLE_PAGES_PER_STEP
    ppb = MOBA_BLOCK // page
    assert page == CHUNK and NP % (2 * pp) == 0 and pp % ppb == 0
    assert NP // ppb >= MOBA_TOPK
    grid_spec = pltpu.PrefetchScalarGridSpec(
        num_scalar_prefetch=1,
        grid=(DB,),
        in_specs=[pl.BlockSpec((1, MOBA_HEADS, KV_W), lambda b, pt: (b, 0, 0)),
                  pl.BlockSpec(memory_space=pl.ANY)],
        out_specs=pl.BlockSpec((1, MOBA_HEADS, LANES), lambda b, pt: (b, 0, 0)),
        scratch_shapes=[pltpu.VMEM((2, pp, KV_W, page), F32), pltpu.SemaphoreType.DMA((2,))],
    )
    return pl.pallas_call(
        functools.partial(_moba_sample_pick_kernel, layer=layer, pp=pp, nc=NP // pp),
        grid_spec=grid_spec,
        out_shape=jax.ShapeDtypeStruct((DB, MOBA_HEADS, LANES), jnp.int32),
        compiler_params=_cparams(("arbitrary",)),
        name="moba_sample_pick",
    )(page_table, q_slot, cache_k_t)


def _moba_sample_attend_kernel(ph_ref, q_ref, kn_ref, vn_ref, k_hbm, v_hbm, o_ref, k_buf, v_buf, sem,
                               *, layer, npg, nc):
    nh = MOBA_HEADS
    hpc = nh // nc
    pp = hpc * npg

    def page_copies(pg, r, slot):
        return [pltpu.make_async_copy(k_hbm.at[layer, pg], k_buf.at[slot, r], sem.at[slot, 0]),
                pltpu.make_async_copy(v_hbm.at[layer, pg], v_buf.at[slot, r], sem.at[slot, 1])]

    q = q_ref[0]
    qb = q.astype(BF16)
    s_own = jnp.sum(q * kn_ref[0], axis=1, keepdims=True)
    v_own = vn_ref[0]
    row = lax.broadcasted_iota(jnp.int32, (nh, npg * CHUNK), 0)
    orow = lax.broadcasted_iota(jnp.int32, (nh, KV_W), 0)

    def compute(slot, c, out):
        s = jnp.zeros((nh, npg * CHUNK), F32)
        for j in range(hpc):
            k_t = jnp.concatenate([k_buf[slot, j * npg + r].astype(BF16) for r in range(npg)], axis=1)
            s = jnp.where(row == c * hpc + j, _dot(qb, k_t), s)
        m = jnp.maximum(jnp.max(s, axis=1, keepdims=True), s_own)
        p = jnp.exp(s - m)
        p_own = jnp.exp(s_own - m)
        l = jnp.sum(p, axis=1, keepdims=True) + p_own
        pb = p.astype(BF16)
        acc = p_own * v_own
        for j in range(hpc):
            v_t = jnp.concatenate([v_buf[slot, j * npg + r].astype(BF16) for r in range(npg)], axis=1)
            acc = acc + jnp.where(orow == c * hpc + j, _dot_nt(pb, v_t), 0.0)
        mine = (orow >= c * hpc) & (orow < (c + 1) * hpc)
        return jnp.where(mine, acc / l, out)

    page_id = lambda sq, n: ph_ref[sq * (nh * npg) + n]
    o_ref[0] = _paged_chunks(page_id, pp, nc, page_copies, compute, jnp.zeros((nh, KV_W), F32))


def _moba_sample_attend(phys, q_slot, k_new, v_new, cache_k_t, cache_v_t, layer):
    DB, H = q_slot.shape[:2]
    page = cache_k_t.shape[3]
    npg = MOBA_TOPK * (MOBA_BLOCK // page)
    nc = 2
    pp = (H // nc) * npg
    grid_spec = pltpu.PrefetchScalarGridSpec(
        num_scalar_prefetch=1,
        grid=(DB,),
        in_specs=[pl.BlockSpec((1, H, KV_W), lambda b, ph: (b, 0, 0)),
                  pl.BlockSpec((1, 1, KV_W), lambda b, ph: (b, 0, 0)),
                  pl.BlockSpec((1, 1, KV_W), lambda b, ph: (b, 0, 0)),
                  pl.BlockSpec(memory_space=pl.ANY),
                  pl.BlockSpec(memory_space=pl.ANY)],
        out_specs=pl.BlockSpec((1, H, KV_W), lambda b, ph: (b, 0, 0)),
        scratch_shapes=[pltpu.VMEM((2, pp, KV_W, page), F32), pltpu.VMEM((2, pp, KV_W, page), F32),
                        pltpu.SemaphoreType.DMA((2, 2))],
    )
    return pl.pallas_call(
        functools.partial(_moba_sample_attend_kernel, layer=layer, npg=npg, nc=nc),
        grid_spec=grid_spec,
        out_shape=jax.ShapeDtypeStruct((DB, H, KV_W), F32),
        compiler_params=_cparams(("arbitrary",)),
        name="moba_sample_attend",
    )(phys, q_slot, k_new, v_new, cache_k_t, cache_v_t)


def _merge_kernel(um_ref, ab_ref, ol_ref, g0_ref, g1_ref, g2_ref, wa_ref, wb_ref, wc_ref, wuv_ref, o_ref):
    y_a = _dot(um_ref[...], wa_ref[...])
    y_b = _dot(ab_ref[...], wb_ref[...])
    ol = ol_ref[...]
    oc = jnp.concatenate(
        [_dot(ol[:, h * MLA_KV_LORA:(h + 1) * MLA_KV_LORA], wuv_ref[h]) for h in range(MLA_HEADS)],
        axis=1).astype(BF16)
    y_c = _dot(oc, wc_ref[...])
    o_ref[...] = (g0_ref[...] * y_a + g1_ref[...] * y_b + g2_ref[...] * y_c).astype(o_ref.dtype)


def _merge(um, attn_b, o_lat, gates, w_a, w_b, w_c, wuv_t, *, tm):
    T = um.shape[0]
    D = w_a.shape[1]
    row = lambda w: pl.BlockSpec((tm, w), lambda i: (i, 0))
    full = lambda a: pl.BlockSpec(a.shape, lambda i: (0,) * a.ndim)
    return pl.pallas_call(
        _merge_kernel,
        grid=(T // tm,),
        in_specs=[row(um.shape[1]), row(attn_b.shape[1]), row(o_lat.shape[1]),
                  pl.BlockSpec((tm, D), lambda i: (i, 0)),
                  pl.BlockSpec((tm, D), lambda i: (i, 1)),
                  pl.BlockSpec((tm, D), lambda i: (i, 2)),
                  full(w_a), full(w_b), full(w_c), full(wuv_t)],
        out_specs=pl.BlockSpec((tm, D), lambda i: (i, 0)),
        out_shape=jax.ShapeDtypeStruct((T, D), BF16),
        compiler_params=_cparams(("parallel",)),
        name="merge",
    )(um, attn_b, o_lat, gates, gates, gates, w_a, w_b, w_c, wuv_t)


def _out_proj_kernel(x_ref, w_ref, h_ref, g_ref, o_ref):
    o_ref[...] = h_ref[...] + _rms(_dot(x_ref[...], w_ref[...]), g_ref[...])


def _out_proj(x, w, h, post_g, *, tm):
    T, D = h.shape
    return pl.pallas_call(
        _out_proj_kernel,
        grid=(T // tm,),
        in_specs=[pl.BlockSpec((tm, x.shape[1]), lambda i: (i, 0)),
                  pl.BlockSpec(w.shape, lambda i: (0, 0)),
                  pl.BlockSpec((tm, D), lambda i: (i, 0)),
                  pl.BlockSpec((1, D), lambda i: (0, 0))],
        out_specs=pl.BlockSpec((tm, D), lambda i: (i, 0)),
        out_shape=jax.ShapeDtypeStruct((T, D), F32),
        compiler_params=_cparams(("parallel",)),
        name="out_proj",
    )(x, w, h, post_g)


def _rot_half_cols(w):
    half = MLA_ROPE // 2
    return jnp.concatenate([-w[..., half:], w[..., :half]], axis=-1)


def _pad_lanes(w):
    return jnp.pad(w, [(0, 0)] * (w.ndim - 1) + [(0, LANES - w.shape[-1])])


def _pack_mix_w_in(w):
    offs = np.cumsum([0, 2 * A_WIDTH, Q_W, KV_W, KV_W, MLA_Q_LORA, MLA_KV_LORA, MLA_ROPE])
    uv, q, k, v, cq, ckv, kr = [w[:, offs[n]:offs[n + 1]] for n in range(7)]
    gate = w[:, offs[7]:]
    small = jnp.concatenate([uv, q, cq, ckv, k, v, _pad_lanes(kr), _pad_lanes(_rot_half_cols(kr))], axis=1)
    assert small.shape[1] == N_SMALL
    return small.astype(BF16), gate.astype(BF16)


def _pack_w_uq(w):
    d = w.shape[0]
    w3 = w.reshape(d, MLA_HEADS, MLA_NOPE + MLA_ROPE)
    nope = w3[:, :, :MLA_NOPE].reshape(d, MLA_HEADS * MLA_NOPE)
    rope = w3[:, :, MLA_NOPE:]
    slot = lambda x: _pad_lanes(x).reshape(d, MLA_HEADS * LANES)
    return jnp.concatenate([nope, slot(rope), slot(_rot_half_cols(rope))], axis=1).astype(BF16)


def _rope_tables(pos):
    half = MLA_ROPE // 2
    inv = ROPE_BASE ** (-jnp.arange(half, dtype=F32) / half)
    ang = pos.astype(F32)[:, None] * inv
    cos, sin = jnp.cos(ang), jnp.sin(ang)
    return (_pad_lanes(jnp.concatenate([cos, cos], axis=1)),
            _pad_lanes(jnp.concatenate([sin, sin], axis=1)))


def kernel(x_prompt, x_sample, cache_moba_k, cache_moba_v, cache_mla_ckv, cache_mla_krope, page_table, ffn1_pre_g, ffn1_w_in, ffn1_w_out, ffn1_post_g, mix_pre_g, mix_w_in, mix_gate_b, sgu_ln_g, sgu_ln_b, sgu_w, sgu_b, sgu_proj, moba_proj, mla_q_norm_g, mla_w_uq, mla_kv_norm_g, mla_w_uk, mla_w_uv, mla_proj, mix_w_out, mix_post_g, ffn2_pre_g, ffn2_w_in, ffn2_w_out, ffn2_post_g):
    B, S, D = x_prompt.shape
    DB, dec_seq, _ = x_sample.shape
    depth = ffn1_w_in.shape[0]
    F = ffn1_w_out.shape[1]
    n_pool, page = cache_moba_k.shape[1:3]
    NP = page_table.shape[1]
    past = NP * page
    Tp = B * S
    T = Tp + DB
    assert dec_seq == 1 and DB % CHUNK == 0 and S % MOBA_BLOCK == 0 and past % MOBA_BLOCK == 0

    tm = _pick_tile(T, 640)
    tf = _pick_tile(F, 512)
    tn_small = _pick_tile(N_SMALL, 512)
    tn_gate = _pick_tile(N_BRANCH * D, 512)
    tm_merge = _pick_tile(T, 320)
    mla_tq = min(256, S)
    mla_tk = _pick_tile(S, 512)
    moba_tk = _pick_tile(S, 512, MOBA_BLOCK)

    cache_k_t = jnp.transpose(cache_moba_k, (0, 1, 3, 4, 2)).reshape(depth, n_pool, KV_W, page)
    cache_v_t = jnp.transpose(cache_moba_v, (0, 1, 3, 4, 2)).reshape(depth, n_pool, KV_W, page)
    cache_kr_t = jnp.swapaxes(cache_mla_krope, 2, 3)

    pos = jnp.concatenate([jnp.tile(jnp.arange(S, dtype=jnp.int32), B),
                           past + jnp.zeros((DB,), jnp.int32)])
    cos, sin = _rope_tables(pos)
    row = lambda a: a.reshape(1, -1)
    eye = jnp.eye(CHUNK, dtype=F32)
    tril = jnp.tril(jnp.ones((CHUNK, CHUNK), dtype=bool))
    pt_flat = page_table.astype(jnp.int32)

    h = jnp.concatenate([x_prompt.reshape(Tp, D), x_sample.reshape(DB, D)], axis=0)
    outs = [[] for _ in range(9)]
    for l in range(depth):
        h = _ffn_half(h, row(ffn1_pre_g[l]), ffn1_w_in[l].astype(BF16), ffn1_w_out[l].astype(BF16),
                      row(ffn1_post_g[l]), tm=tm, tf=tf)

        w_small, w_gate = _pack_mix_w_in(mix_w_in[l])
        pre_g = row(mix_pre_g[l])
        z = _norm_linear(h, pre_g, w_small, tm=tm, tn=tn_small, name="in_proj")
        gates = _norm_linear(h, pre_g, w_gate, row(mix_gate_b[l]), tm=tm, tn=tn_gate, name="gate_proj")

        w_s = sgu_w[l]
        mix_w = jnp.stack([jnp.where(tril, w_s, 0.0), w_s[:, 0, 0][:, None, None] * eye]).astype(BF16)
        b_s = sgu_b[l]
        mix_b = jnp.stack([jnp.repeat(b_s.T, A_GROUP_W, axis=1),
                           jnp.broadcast_to(jnp.repeat(b_s[:, 0], A_GROUP_W)[None], (CHUNK, A_WIDTH))])
        um, v_rows = _gmlp(z, row(sgu_ln_g[l]), row(sgu_ln_b[l]), mix_w, mix_b, n_prompt_chunks=Tp // CHUNK)

        wuk_t = jnp.transpose(mla_w_uk[l], (1, 2, 0)).astype(BF16)
        q_mla, ckv, krope, kc = _mla_prep(z, cos, sin, row(mla_q_norm_g[l]), row(mla_kv_norm_g[l]),
                                          _pack_w_uq(mla_w_uq[l]), wuk_t, tm=tm)

        q_aug, k_aug = _moba_select(z, B=B, S=S)
        v_t = z[:Tp, OFF_V:OFF_V + KV_W].T.reshape(MOBA_KV_HEADS, MOBA_HEAD_DIM, Tp).astype(BF16)
        attn_p = _flash(q_aug, k_aug, v_t, B=B, S=S, nh=MOBA_GROUP, tq=MOBA_BLOCK, tk=moba_tk,
                        qw=LANES, dv=MOBA_HEAD_DIM, name="moba_prompt")
        ckv_t = kc[:Tp, :MLA_KV_LORA].T[None]
        olat_p = _flash(q_mla, kc[None], ckv_t, B=B, S=S, nh=MLA_HEADS, tq=mla_tq, tk=mla_tk,
                        qw=MLA_QW, dv=MLA_KV_LORA, name="mla_prompt")

        zs = z[Tp:]
        q_s = zs[:, OFF_Q:OFF_Q + Q_W].reshape(DB, MOBA_HEADS, MOBA_HEAD_DIM)
        k_new = zs[:, OFF_K:OFF_K + KV_W]
        v_new = zs[:, OFF_V:OFF_V + KV_W]
        kvh = jnp.arange(MOBA_HEADS) // MOBA_GROUP
        lane_kvh = jnp.arange(KV_W) // MOBA_HEAD_DIM
        q_slot = jnp.where(lane_kvh[None, None, :] == kvh[None, :, None],
                           jnp.tile(q_s * MOBA_SCALE, (1, 1, MOBA_KV_HEADS)), 0.0)
        top = _moba_sample_pick(pt_flat, q_slot, cache_k_t, l)[:, :, :MOBA_TOPK]
        ppb = MOBA_BLOCK // page
        logical = (top[..., None] * ppb + jnp.arange(ppb, dtype=jnp.int32)).reshape(DB, -1)
        phys = jnp.take_along_axis(pt_flat, logical, axis=1).reshape(-1)
        o_slot = _moba_sample_attend(phys, q_slot, k_new.reshape(DB, 1, KV_W), v_new.reshape(DB, 1, KV_W),
                                     cache_k_t, cache_v_t, l)
        o_slot = o_slot.reshape(DB, MOBA_HEADS, MOBA_KV_HEADS, MOBA_HEAD_DIM)
        attn_s = jnp.concatenate([o_slot[:, hh, hh // MOBA_GROUP] for hh in range(MOBA_HEADS)], axis=1)
        kc_new = jnp.concatenate([ckv[Tp:], krope[Tp:]], axis=1).reshape(DB, 1, MLA_QW)
        olat_s = _mla_sample(pt_flat, q_mla[Tp:].reshape(DB, MLA_HEADS, MLA_QW), kc_new,
                             cache_mla_ckv, cache_kr_t, l)

        attn_b = jnp.concatenate([attn_p, attn_s.astype(BF16)], axis=0)
        o_lat = jnp.concatenate([olat_p, olat_s.reshape(DB, MLA_HEADS * MLA_KV_LORA)], axis=0)
        wuv_t = jnp.transpose(mla_w_uv[l], (1, 0, 2)).astype(BF16)
        merged = _merge(um, attn_b, o_lat, gates, sgu_proj[l].astype(BF16), moba_proj[l].astype(BF16),
                        mla_proj[l].astype(BF16), wuv_t, tm=tm_merge)
        h = _out_proj(merged, mix_w_out[l].astype(BF16), h, row(mix_post_g[l]), tm=tm)

        h = _ffn_half(h, row(ffn2_pre_g[l]), ffn2_w_in[l].astype(BF16), ffn2_w_out[l].astype(BF16),
                      row(ffn2_post_g[l]), tm=tm, tf=tf)

        k_all = z[:, OFF_K:OFF_K + KV_W]
        v_all = z[:, OFF_V:OFF_V + KV_W]
        kr_all = krope[:, :MLA_ROPE]
        outs[0].append(k_all[:Tp].reshape(B, S, MOBA_KV_HEADS, MOBA_HEAD_DIM))
        outs[1].append(v_all[:Tp].reshape(B, S, MOBA_KV_HEADS, MOBA_HEAD_DIM))
        outs[2].append(ckv[:Tp].reshape(B, S, MLA_KV_LORA))
        outs[3].append(kr_all[:Tp].reshape(B, S, MLA_ROPE))
        outs[4].append(k_all[Tp:].reshape(DB, 1, MOBA_KV_HEADS, MOBA_HEAD_DIM))
        outs[5].append(v_all[Tp:].reshape(DB, 1, MOBA_KV_HEADS, MOBA_HEAD_DIM))
        outs[6].append(ckv[Tp:].reshape(DB, 1, MLA_KV_LORA))
        outs[7].append(kr_all[Tp:].reshape(DB, 1, MLA_ROPE))
        outs[8].append(v_rows[Tp:].reshape(DB, 1, A_WIDTH))

    return (h[:Tp].reshape(B, S, D), h[Tp:].reshape(DB, 1, D)) + tuple(jnp.stack(o) for o in outs)
```

```python
import functools

import numpy as np
import jax
import jax.numpy as jnp
from jax import lax
from jax.experimental import pallas as pl
from jax.experimental.pallas import tpu as pltpu

F32 = jnp.float32
BF16 = jnp.bfloat16

CHUNK = 128
A_GROUPS = 4
A_GROUP_W = 128
A_WIDTH = A_GROUPS * A_GROUP_W
MOBA_HEADS = 8
MOBA_KV_HEADS = 2
MOBA_GROUP = MOBA_HEADS // MOBA_KV_HEADS
MOBA_HEAD_DIM = 64
MOBA_BLOCK = 256
MOBA_TOPK = 3
MLA_HEADS = 8
MLA_NOPE = 64
MLA_ROPE = 32
MLA_V = 64
MLA_Q_LORA = 384
MLA_KV_LORA = 128
MLA_SCALE = (MLA_NOPE + MLA_ROPE) ** -0.5
MOBA_SCALE = MOBA_HEAD_DIM ** -0.5
ROPE_BASE = 10000.0
N_BRANCH = 3
EPS = 1e-6
NEG = -1e30

LANES = 128
VMEM_LIMIT = 56 * 1024 * 1024
MASK_BIG = 2.0 ** 100
PICK_NEG = -3.0e38

KV_W = MOBA_KV_HEADS * MOBA_HEAD_DIM
Q_W = MOBA_HEADS * MOBA_HEAD_DIM
OFF_UV = 0
OFF_Q = OFF_UV + 2 * A_WIDTH
OFF_CQ = OFF_Q + Q_W
OFF_CKV = OFF_CQ + MLA_Q_LORA
OFF_K = OFF_CKV + MLA_KV_LORA
OFF_V = OFF_K + KV_W
OFF_KR = OFF_V + KV_W
OFF_KRROT = OFF_KR + LANES
N_SMALL = OFF_KRROT + LANES
MLA_QW = 2 * LANES
SAMPLE_PAGES_PER_STEP = 32


def _cparams(sem, vmem=VMEM_LIMIT):
    return pltpu.CompilerParams(dimension_semantics=sem, vmem_limit_bytes=vmem)


def _pick_tile(n, cap, mult=LANES):
    best = None
    t = mult
    while t <= min(n, cap):
        if n % t == 0:
            best = t
        t += mult
    assert best is not None, (n, cap, mult)
    return best


def _rms(x, g):
    ms = jnp.mean(x * x, axis=-1, keepdims=True)
    return x * lax.rsqrt(ms + EPS) * g


def _sigmoid(x):
    return 1.0 / (1.0 + jnp.exp(-x))


def _dot(a, b):
    return jnp.dot(a, b, preferred_element_type=F32)


def _dot_nt(a, b, precision=None):
    return lax.dot_general(a, b, (((1,), (1,)), ((), ())), precision=precision,
                           preferred_element_type=F32)


def _ffn_kernel(h_ref, pre_ref, wg_ref, wu_ref, wo_ref, post_ref, o_ref, xn_ref, acc_ref, *, nj):
    j = pl.program_id(1)

    @pl.when(j == 0)
    def _():
        xn_ref[...] = _rms(h_ref[...], pre_ref[...]).astype(BF16)
        acc_ref[...] = jnp.zeros_like(acc_ref)

    xn = xn_ref[...]
    gate = _dot(xn, wg_ref[...])
    up = _dot(xn, wu_ref[...])
    act = (gate * _sigmoid(gate) * up).astype(BF16)
    acc_ref[...] += _dot(act, wo_ref[...])

    @pl.when(j == nj - 1)
    def _():
        o_ref[...] = h_ref[...] + 0.5 * _rms(acc_ref[...], post_ref[...])


def _ffn_half(h, pre_g, w_in, w_out, post_g, *, tm, tf):
    T, D = h.shape
    F = w_out.shape[0]
    nj = F // tf
    return pl.pallas_call(
        functools.partial(_ffn_kernel, nj=nj),
        grid=(T // tm, nj),
        in_specs=[
            pl.BlockSpec((tm, D), lambda i, j: (i, 0)),
            pl.BlockSpec((1, D), lambda i, j: (0, 0)),
            pl.BlockSpec((D, tf), lambda i, j: (0, j)),
            pl.BlockSpec((D, tf), lambda i, j: (0, j + nj)),
            pl.BlockSpec((tf, D), lambda i, j: (j, 0)),
            pl.BlockSpec((1, D), lambda i, j: (0, 0)),
        ],
        out_specs=pl.BlockSpec((tm, D), lambda i, j: (i, 0)),
        out_shape=jax.ShapeDtypeStruct((T, D), F32),
        scratch_shapes=[pltpu.VMEM((tm, D), BF16), pltpu.VMEM((tm, D), F32)],
        compiler_params=_cparams(("parallel", "arbitrary")),
        name="ffn_half",
    )(h, pre_g, w_in, w_in, w_out, post_g)


def _norm_linear_kernel(x_ref, g_ref, w_ref, *rest, gate):
    if gate:
        b_ref, o_ref, xn_ref = rest
    else:
        o_ref, xn_ref = rest

    @pl.when(pl.program_id(1) == 0)
    def _():
        xn_ref[...] = _rms(x_ref[...], g_ref[...]).astype(BF16)

    y = _dot(xn_ref[...], w_ref[...])
    if gate:
        y = _sigmoid(y + b_ref[...])
    o_ref[...] = y.astype(o_ref.dtype)


def _norm_linear(x, g, w, bias=None, *, tm, tn, name):
    T, D = x.shape
    N = w.shape[1]
    in_specs = [
        pl.BlockSpec((tm, D), lambda i, j: (i, 0)),
        pl.BlockSpec((1, D), lambda i, j: (0, 0)),
        pl.BlockSpec((D, tn), lambda i, j: (0, j)),
    ]
    args = [x, g, w]
    if bias is not None:
        in_specs.append(pl.BlockSpec((1, tn), lambda i, j: (0, j)))
        args.append(bias)
    return pl.pallas_call(
        functools.partial(_norm_linear_kernel, gate=bias is not None),
        grid=(T // tm, N // tn),
        in_specs=in_specs,
        out_specs=pl.BlockSpec((tm, tn), lambda i, j: (i, j)),
        out_shape=jax.ShapeDtypeStruct((T, N), F32),
        scratch_shapes=[pltpu.VMEM((tm, D), BF16)],
        compiler_params=_cparams(("parallel", "arbitrary")),
        name=name,
    )(*args)


def _gelu_tanh(x):
    c = np.sqrt(2.0 / np.pi).astype(np.float32)
    return 0.5 * x * (1.0 + jnp.tanh(c * (x + 0.044715 * (x * x * x))))


def _gmlp_kernel(z_ref, lng_ref, lnb_ref, w_ref, b_ref, um_ref, v_ref):
    a = _gelu_tanh(z_ref[...])
    u = a[:, :A_WIDTH]
    v = a[:, A_WIDTH:]
    mu = jnp.mean(v, axis=-1, keepdims=True)
    vc = v - mu
    var = jnp.mean(vc * vc, axis=-1, keepdims=True)
    vn = vc * lax.rsqrt(var + EPS) * lng_ref[...] + lnb_ref[...]
    v_ref[...] = vn
    vb = vn.astype(BF16)
    mixed = jnp.concatenate(
        [_dot(w_ref[0, g], vb[:, g * A_GROUP_W:(g + 1) * A_GROUP_W]) for g in range(A_GROUPS)], axis=1)
    um_ref[...] = (u * (mixed + b_ref[0])).astype(BF16)


def _gmlp(z_small, ln_g, ln_b, mix_w, mix_b, *, n_prompt_chunks):
    T = z_small.shape[0]
    n_chunks = T // CHUNK
    mode = lambda c: c // n_prompt_chunks
    return pl.pallas_call(
        _gmlp_kernel,
        grid=(n_chunks,),
        in_specs=[
            pl.BlockSpec((CHUNK, 2 * A_WIDTH), lambda c: (c, OFF_UV // (2 * A_WIDTH))),
            pl.BlockSpec((1, A_WIDTH), lambda c: (0, 0)),
            pl.BlockSpec((1, A_WIDTH), lambda c: (0, 0)),
            pl.BlockSpec((1, A_GROUPS, CHUNK, CHUNK), lambda c: (mode(c), 0, 0, 0)),
            pl.BlockSpec((1, CHUNK, A_WIDTH), lambda c: (mode(c), 0, 0)),
        ],
        out_specs=[pl.BlockSpec((CHUNK, A_WIDTH), lambda c: (c, 0)),
                   pl.BlockSpec((CHUNK, A_WIDTH), lambda c: (c, 0))],
        out_shape=[jax.ShapeDtypeStruct((T, A_WIDTH), BF16),
                   jax.ShapeDtypeStruct((T, A_WIDTH), F32)],
        compiler_params=_cparams(("parallel",)),
        name="gmlp_chunk",
    )(z_small, ln_g, ln_b, mix_w, mix_b)


def _mla_prep_kernel(zc_ref, zk_ref, cos_ref, sin_ref, qg_ref, kvg_ref, wuq_ref, wuk_ref,
                     q_ref, ckv_ref, kr_ref, kc_ref):
    zc = zc_ref[...]
    cqn = _rms(zc[:, :MLA_Q_LORA], qg_ref[...]).astype(BF16)
    q = _dot(cqn, wuq_ref[...])
    nope_w = MLA_HEADS * MLA_NOPE
    slot_w = MLA_HEADS * LANES
    cos = cos_ref[...]
    sin = sin_ref[...]
    cos_h = jnp.concatenate([cos] * MLA_HEADS, axis=1)
    sin_h = jnp.concatenate([sin] * MLA_HEADS, axis=1)
    q_rope = q[:, nope_w:nope_w + slot_w] * cos_h + q[:, nope_w + slot_w:] * sin_h
    for h in range(MLA_HEADS):
        q_nope = q[:, h * MLA_NOPE:(h + 1) * MLA_NOPE].astype(BF16)
        q_lat = _dot(q_nope, wuk_ref[h])
        q_ref[:, h * MLA_QW:h * MLA_QW + LANES] = (q_lat * MLA_SCALE).astype(BF16)
        q_ref[:, h * MLA_QW + LANES:(h + 1) * MLA_QW] = (
            q_rope[:, h * LANES:(h + 1) * LANES] * MLA_SCALE).astype(BF16)
    ckv = _rms(zc[:, MLA_Q_LORA:], kvg_ref[...])
    zk = zk_ref[...]
    kr = zk[:, :LANES] * cos + zk[:, LANES:] * sin
    ckv_ref[...] = ckv
    kr_ref[...] = kr
    kc_ref[:, :LANES] = ckv.astype(BF16)
    kc_ref[:, LANES:] = kr.astype(BF16)


def _mla_prep(z_small, cos, sin, q_norm_g, kv_norm_g, wuq, wuk_t, *, tm):
    T = z_small.shape[0]
    cw = MLA_Q_LORA + MLA_KV_LORA
    return pl.pallas_call(
        _mla_prep_kernel,
        grid=(T // tm,),
        in_specs=[
            pl.BlockSpec((tm, cw), lambda i: (i, OFF_CQ // cw)),
            pl.BlockSpec((tm, 2 * LANES), lambda i: (i, OFF_KR // (2 * LANES))),
            pl.BlockSpec((tm, LANES), lambda i: (i, 0)),
            pl.BlockSpec((tm, LANES), lambda i: (i, 0)),
            pl.BlockSpec((1, MLA_Q_LORA), lambda i: (0, 0)),
            pl.BlockSpec((1, MLA_KV_LORA), lambda i: (0, 0)),
            pl.BlockSpec(wuq.shape, lambda i: (0, 0)),
            pl.BlockSpec(wuk_t.shape, lambda i: (0, 0, 0)),
        ],
        out_specs=[
            pl.BlockSpec((tm, MLA_HEADS * MLA_QW), lambda i: (i, 0)),
            pl.BlockSpec((tm, LANES), lambda i: (i, 0)),
            pl.BlockSpec((tm, LANES), lambda i: (i, 0)),
            pl.BlockSpec((tm, 2 * LANES), lambda i: (i, 0)),
        ],
        out_shape=[
            jax.ShapeDtypeStruct((T, MLA_HEADS * MLA_QW), BF16),
            jax.ShapeDtypeStruct((T, LANES), F32),
            jax.ShapeDtypeStruct((T, LANES), F32),
            jax.ShapeDtypeStruct((T, 2 * LANES), BF16),
        ],
        compiler_params=_cparams(("parallel",)),
        name="mla_prep",
    )(z_small, z_small, cos, sin, q_norm_g, kv_norm_g, wuq, wuk_t)


def _topk_lanes(sc, lane_f, k):
    picks = []
    for _ in range(k):
        m = jnp.max(sc, axis=1, keepdims=True)
        idx = jnp.min(jnp.where(sc == m, lane_f, 1e9), axis=1, keepdims=True)
        hit = lane_f == idx
        picks.append((idx, m > 0.5 * PICK_NEG, hit))
        sc = jnp.where(hit, PICK_NEG, sc)
    return picks


def _moba_select_kernel(q_ref, k_ref, qa_ref, ka_ref, means_ref, *, nb):
    i = pl.program_id(1)
    hd = MOBA_HEAD_DIM
    tq = q_ref.shape[0]

    @pl.when(i == 0)
    def _():
        means_ref[...] = jnp.zeros_like(means_ref)

    q = q_ref[...]
    k = k_ref[...]
    means = means_ref[0:nb, :]
    blk = lax.broadcasted_iota(jnp.int32, (tq, nb), 1)
    blk_f = blk.astype(F32)
    past = blk < i
    pad = jnp.zeros((tq, LANES - hd - nb), F32)
    pieces = []
    for h in range(MOBA_HEADS):
        g = h // MOBA_GROUP
        qh = q[:, h * hd:(h + 1) * hd]
        sc = _dot_nt(qh, means[:, g * hd:(g + 1) * hd], precision=lax.Precision.HIGHEST)
        sc = jnp.where(past, sc, PICK_NEG)
        chosen = blk == i
        for _, ok, hit in _topk_lanes(sc, blk_f, min(MOBA_TOPK, nb)):
            chosen = chosen | (hit & ok)
        pieces += [qh * MOBA_SCALE, jnp.where(chosen, 0.0, 1.0), pad]
    qa_ref[...] = jnp.concatenate(pieces, axis=1).astype(BF16)
    onehot = jnp.where(blk == i, -MASK_BIG, 0.0)
    for g in range(MOBA_KV_HEADS):
        ka_ref[g] = jnp.concatenate([k[:, g * hd:(g + 1) * hd], onehot, pad], axis=1).astype(BF16)
    means_ref[pl.ds(i, 1), :] = jnp.sum(k, axis=0, keepdims=True) * (1.0 / MOBA_BLOCK)


def _moba_select(z_small, *, B, S):
    nb = S // MOBA_BLOCK
    Tp = B * S
    tq = MOBA_BLOCK
    assert MOBA_HEAD_DIM + nb <= LANES
    return pl.pallas_call(
        functools.partial(_moba_select_kernel, nb=nb),
        grid=(B, nb),
        in_specs=[
            pl.BlockSpec((tq, Q_W), lambda b, i: (b * nb + i, OFF_Q // Q_W)),
            pl.BlockSpec((tq, KV_W), lambda b, i: (b * nb + i, OFF_K // KV_W)),
        ],
        out_specs=[
            pl.BlockSpec((tq, MOBA_HEADS * LANES), lambda b, i: (b * nb + i, 0)),
            pl.BlockSpec((MOBA_KV_HEADS, tq, LANES), lambda b, i: (0, b * nb + i, 0)),
        ],
        out_shape=[
            jax.ShapeDtypeStruct((Tp, MOBA_HEADS * LANES), BF16),
            jax.ShapeDtypeStruct((MOBA_KV_HEADS, Tp, LANES), BF16),
        ],
        scratch_shapes=[pltpu.VMEM((max(nb, 8), KV_W), F32)],
        compiler_params=_cparams(("arbitrary", "arbitrary")),
        name="moba_select",
    )(z_small, z_small)


def _flash_kernel(q_ref, k_ref, vt_ref, o_ref, qt_ref, m_ref, l_ref, acc_ref, *, nh, tq, tk, qw, dv):
    qi = pl.program_id(2)
    kj = pl.program_id(3)
    last = (qi * tq + (tq - 1)) // tk
    cols = nh * tq

    @pl.when(kj == 0)
    def _():
        for h in range(nh):
            qt_ref[:, h * tq:(h + 1) * tq] = q_ref[:, h * qw:(h + 1) * qw].astype(F32).T.astype(BF16)
        m_ref[...] = jnp.full_like(m_ref, -jnp.inf)
        l_ref[...] = jnp.zeros_like(l_ref)
        acc_ref[...] = jnp.zeros_like(acc_ref)

    def step(masked):
        s = _dot(k_ref[0], qt_ref[...])
        if masked:
            kpos = kj * tk + lax.broadcasted_iota(jnp.int32, (tk, cols), 0)
            qpos = qi * tq + (lax.broadcasted_iota(jnp.int32, (tk, cols), 1) & (tq - 1))
            s = jnp.where(kpos <= qpos, s, NEG)
        m_prev = m_ref[...]
        m_new = jnp.maximum(m_prev, jnp.max(s, axis=0, keepdims=True))
        alpha = jnp.exp(m_prev - m_new)
        p = jnp.exp(s - m_new)
        l_ref[...] = alpha * l_ref[...] + jnp.sum(p, axis=0, keepdims=True)
        acc_ref[...] = alpha * acc_ref[...] + _dot(vt_ref[0], p.astype(BF16))
        m_ref[...] = m_new

    @pl.when(kj < last)
    def _():
        step(False)

    @pl.when(kj == last)
    def _():
        step(True)
        out_t = acc_ref[...] / l_ref[...]
        stacked = jnp.concatenate([out_t[:, h * tq:(h + 1) * tq] for h in range(nh)], axis=0)
        o_ref[...] = stacked.T.astype(o_ref.dtype)


def _flash(q, k, vt, *, B, S, nh, tq, tk, qw, dv, name):
    G = k.shape[0]
    nq, nk = S // tq, S // tk
    assert tq & (tq - 1) == 0 and S % tq == 0 and S % tk == 0
    last_k = lambda b, qi, kj: b * nk + jnp.minimum(kj, (qi * tq + (tq - 1)) // tk)
    return pl.pallas_call(
        functools.partial(_flash_kernel, nh=nh, tq=tq, tk=tk, qw=qw, dv=dv),
        grid=(B, G, nq, nk),
        in_specs=[pl.BlockSpec((tq, nh * qw), lambda b, g, qi, kj: (b * nq + qi, g)),
                  pl.BlockSpec((1, tk, qw), lambda b, g, qi, kj: (g, last_k(b, qi, kj), 0)),
                  pl.BlockSpec((1, dv, tk), lambda b, g, qi, kj: (g, 0, last_k(b, qi, kj)))],
        out_specs=pl.BlockSpec((tq, nh * dv), lambda b, g, qi, kj: (b * nq + qi, g)),
        out_shape=jax.ShapeDtypeStruct((B * S, G * nh * dv), BF16),
        scratch_shapes=[pltpu.VMEM((qw, nh * tq), BF16), pltpu.VMEM((1, nh * tq), F32),
                        pltpu.VMEM((1, nh * tq), F32), pltpu.VMEM((dv, nh * tq), F32)],
        compiler_params=_cparams(("parallel", "parallel", "parallel", "arbitrary")),
        name=name,
    )(q, k, vt)


def _paged_chunks(page_id, pp, nc, page_copies, compute, carry):
    assert nc % 2 == 0
    seq = pl.program_id(0)
    n_seq = pl.num_programs(0)

    def copies(sq, c):
        return [page_copies(page_id(sq, c * pp + r), r, c % 2) for r in range(pp)]

    def start(sq, c):
        for r, cps in enumerate(copies(sq, c)):
            for cp in cps:
                cp.start(priority=r % 2)

    @pl.when(seq == 0)
    def _():
        start(seq, 0)

    for c in range(nc):
        if c + 1 < nc:
            start(seq, c + 1)
        else:
            @pl.when(seq + 1 < n_seq)
            def _():
                start(seq + 1, 0)
        for cps in copies(seq, c):
            for cp in cps:
                cp.wait()
        carry = compute(c % 2, c, carry)
    return carry


def _mla_sample_kernel(pt_ref, q_ref, kn_ref, ckv_hbm, kr_hbm, o_ref, ckv_buf, kr_buf, sem, *, layer, pp, nc):
    page = CHUNK

    def page_copies(pg, r, slot):
        return [pltpu.make_async_copy(ckv_hbm.at[layer, pg], ckv_buf.at[slot, pl.ds(r * page, page), :],
                                      sem.at[slot, 0]),
                pltpu.make_async_copy(kr_hbm.at[layer, pg], kr_buf.at[slot, :, pl.ds(r * page, page)],
                                      sem.at[slot, 1])]

    q = q_ref[0]
    q_lat = q[:, :MLA_KV_LORA]
    q_rope = q[:, LANES:LANES + MLA_ROPE]
    kn = kn_ref[0]

    def compute(slot, c, carry):
        m_prev, l_prev, acc_prev = carry
        ckv = ckv_buf[slot].astype(BF16)
        kr_t = kr_buf[slot].astype(BF16)
        s = _dot_nt(q_lat, ckv) + _dot(q_rope, kr_t)
        m_new = jnp.maximum(m_prev, jnp.max(s, axis=1, keepdims=True))
        alpha = jnp.exp(m_prev - m_new)
        p = jnp.exp(s - m_new)
        l_new = alpha * l_prev + jnp.sum(p, axis=1, keepdims=True)
        return m_new, l_new, alpha * acc_prev + _dot(p.astype(BF16), ckv)

    init = (jnp.sum(q.astype(F32) * kn, axis=1, keepdims=True),
            jnp.ones((MLA_HEADS, 1), F32),
            jnp.broadcast_to(kn[:, :MLA_KV_LORA], (MLA_HEADS, MLA_KV_LORA)))
    _, l, acc = _paged_chunks(lambda sq, n: pt_ref[sq, n], pp, nc, page_copies, compute, init)
    o_ref[0] = (acc / l).astype(o_ref.dtype)


def _mla_sample(page_table, q_s, kc_new, cache_ckv, cache_kr_t, layer):
    DB, NP = page_table.shape
    page = cache_ckv.shape[2]
    assert page == CHUNK
    pp = SAMPLE_PAGES_PER_STEP
    assert NP % (2 * pp) == 0
    grid_spec = pltpu.PrefetchScalarGridSpec(
        num_scalar_prefetch=1,
        grid=(DB,),
        in_specs=[pl.BlockSpec((1, MLA_HEADS, MLA_QW), lambda b, pt: (b, 0, 0)),
                  pl.BlockSpec((1, 1, MLA_QW), lambda b, pt: (b, 0, 0)),
                  pl.BlockSpec(memory_space=pl.ANY),
                  pl.BlockSpec(memory_space=pl.ANY)],
        out_specs=pl.BlockSpec((1, MLA_HEADS, MLA_KV_LORA), lambda b, pt: (b, 0, 0)),
        scratch_shapes=[pltpu.VMEM((2, pp * page, MLA_KV_LORA), F32),
                        pltpu.VMEM((2, MLA_ROPE, pp * page), F32),
                        pltpu.SemaphoreType.DMA((2, 2))],
    )
    return pl.pallas_call(
        functools.partial(_mla_sample_kernel, layer=layer, pp=pp, nc=NP // pp),
        grid_spec=grid_spec,
        out_shape=jax.ShapeDtypeStruct((DB, MLA_HEADS, MLA_KV_LORA), BF16),
        compiler_params=_cparams(("arbitrary",)),
        name="mla_sample",
    )(page_table, q_s, kc_new, cache_ckv, cache_kr_t)


def _moba_sample_pick_kernel(pt_ref, q_ref, k_hbm, top_ref, k_buf, sem, *, layer, pp, nc):
    ppb = MOBA_BLOCK // CHUNK
    bpc = pp // ppb
    nb = nc * bpc
    col = lax.broadcasted_iota(jnp.int32, (KV_W, nb), 1)

    def page_copies(pg, r, slot):
        return [pltpu.make_async_copy(k_hbm.at[layer, pg], k_buf.at[slot, r], sem.at[slot])]

    def compute(slot, c, sums):
        for n in range(bpc):
            tot = k_buf[slot, n * ppb]
            for r in range(1, ppb):
                tot = tot + k_buf[slot, n * ppb + r]
            sums = jnp.where(col == c * bpc + n, jnp.sum(tot, axis=1, keepdims=True), sums)
        return sums

    sums = _paged_chunks(lambda sq, n: pt_ref[sq, n], pp, nc, page_copies, compute,
                         jnp.zeros((KV_W, nb), F32))
    means_t = sums * (1.0 / MOBA_BLOCK)
    sc = jnp.dot(q_ref[0], means_t, precision=lax.Precision.HIGHEST, preferred_element_type=F32)
    lane_f = lax.broadcasted_iota(jnp.int32, (MOBA_HEADS, nb), 1).astype(F32)
    picks = _topk_lanes(sc, lane_f, MOBA_TOPK)
    out_lane = lax.broadcasted_iota(jnp.int32, (MOBA_HEADS, LANES), 1)
    out = jnp.zeros((MOBA_HEADS, LANES), F32)
    for t, (idx, _, _) in enumerate(picks):
        out = jnp.where(out_lane == t, idx, out)
    top_ref[0] = out.astype(jnp.int32)


def _moba_sample_pick(page_table, q_slot, cache_k_t, layer):
    DB, NP = page_table.shape
    page = cache_k_t.shape[3]
    pp = SAMPLE_PAGES_PER_STEP
    ppb = MOBA_BLOCK // page
    assert page == CHUNK and NP % (2 * pp) == 0 and pp % ppb == 0
    assert NP // ppb >= MOBA_TOPK
    grid_spec = pltpu.PrefetchScalarGridSpec(
        num_scalar_prefetch=1,
        grid=(DB,),
        in_specs=[pl.BlockSpec((1, MOBA_HEADS, KV_W), lambda b, pt: (b, 0, 0)),
                  pl.BlockSpec(memory_space=pl.ANY)],
        out_specs=pl.BlockSpec((1, MOBA_HEADS, LANES), lambda b, pt: (b, 0, 0)),
        scratch_shapes=[pltpu.VMEM((2, pp, KV_W, page), F32), pltpu.SemaphoreType.DMA((2,))],
    )
    return pl.pallas_call(
        functools.partial(_moba_sample_pick_kernel, layer=layer, pp=pp, nc=NP // pp),
        grid_spec=grid_spec,
        out_shape=jax.ShapeDtypeStruct((DB, MOBA_HEADS, LANES), jnp.int32),
        compiler_params=_cparams(("arbitrary",)),
        name="moba_sample_pick",
    )(page_table, q_slot, cache_k_t)


def _moba_sample_attend_kernel(ph_ref, q_ref, kn_ref, vn_ref, k_hbm, v_hbm, o_ref, k_buf, v_buf, sem,
                               *, layer, npg, nc):
    nh = MOBA_HEADS
    hpc = nh // nc
    pp = hpc * npg

    def page_copies(pg, r, slot):
        return [pltpu.make_async_copy(k_hbm.at[layer, pg], k_buf.at[slot, r], sem.at[slot, 0]),
                pltpu.make_async_copy(v_hbm.at[layer, pg], v_buf.at[slot, r], sem.at[slot, 1])]

    q = q_ref[0]
    qb = q.astype(BF16)
    s_own = jnp.sum(q * kn_ref[0], axis=1, keepdims=True)
    v_own = vn_ref[0]
    row = lax.broadcasted_iota(jnp.int32, (nh, npg * CHUNK), 0)
    orow = lax.broadcasted_iota(jnp.int32, (nh, KV_W), 0)

    def compute(slot, c, out):
        s = jnp.zeros((nh, npg * CHUNK), F32)
        for j in range(hpc):
            k_t = jnp.concatenate([k_buf[slot, j * npg + r].astype(BF16) for r in range(npg)], axis=1)
            s = jnp.where(row == c * hpc + j, _dot(qb, k_t), s)
        m = jnp.maximum(jnp.max(s, axis=1, keepdims=True), s_own)
        p = jnp.exp(s - m)
        p_own = jnp.exp(s_own - m)
        l = jnp.sum(p, axis=1, keepdims=True) + p_own
        pb = p.astype(BF16)
        acc = p_own * v_own
        for j in range(hpc):
            v_t = jnp.concatenate([v_buf[slot, j * npg + r].astype(BF16) for r in range(npg)], axis=1)
            acc = acc + jnp.where(orow == c * hpc + j, _dot_nt(pb, v_t), 0.0)
        mine = (orow >= c * hpc) & (orow < (c + 1) * hpc)
        return jnp.where(mine, acc / l, out)

    page_id = lambda sq, n: ph_ref[sq * (nh * npg) + n]
    o_ref[0] = _paged_chunks(page_id, pp, nc, page_copies, compute, jnp.zeros((nh, KV_W), F32))


def _moba_sample_attend(phys, q_slot, k_new, v_new, cache_k_t, cache_v_t, layer):
    DB, H = q_slot.shape[:2]
    page = cache_k_t.shape[3]
    npg = MOBA_TOPK * (MOBA_BLOCK // page)
    nc = 2
    pp = (H // nc) * npg
    grid_spec = pltpu.PrefetchScalarGridSpec(
        num_scalar_prefetch=1,
        grid=(DB,),
        in_specs=[pl.BlockSpec((1, H, KV_W), lambda b, ph: (b, 0, 0)),
                  pl.BlockSpec((1, 1, KV_W), lambda b, ph: (b, 0, 0)),
                  pl.BlockSpec((1, 1, KV_W), lambda b, ph: (b, 0, 0)),
                  pl.BlockSpec(memory_space=pl.ANY),
                  pl.BlockSpec(memory_space=pl.ANY)],
        out_specs=pl.BlockSpec((1, H, KV_W), lambda b, ph: (b, 0, 0)),
        scratch_shapes=[pltpu.VMEM((2, pp, KV_W, page), F32), pltpu.VMEM((2, pp, KV_W, page), F32),
                        pltpu.SemaphoreType.DMA((2, 2))],
    )
    return pl.pallas_call(
        functools.partial(_moba_sample_attend_kernel, layer=layer, npg=npg, nc=nc),
        grid_spec=grid_spec,
        out_shape=jax.ShapeDtypeStruct((DB, H, KV_W), F32),
        compiler_params=_cparams(("arbitrary",)),
        name="moba_sample_attend",
    )(phys, q_slot, k_new, v_new, cache_k_t, cache_v_t)


def _merge_kernel(um_ref, ab_ref, ol_ref, g0_ref, g1_ref, g2_ref, wa_ref, wb_ref, wc_ref, wuv_ref, o_ref):
    y_a = _dot(um_ref[...], wa_ref[...])
    y_b = _dot(ab_ref[...], wb_ref[...])
    ol = ol_ref[...]
    oc = jnp.concatenate(
        [_dot(ol[:, h * MLA_KV_LORA:(h + 1) * MLA_KV_LORA], wuv_ref[h]) for h in range(MLA_HEADS)],
        axis=1).astype(BF16)
    y_c = _dot(oc, wc_ref[...])
    o_ref[...] = (g0_ref[...] * y_a + g1_ref[...] * y_b + g2_ref[...] * y_c).astype(o_ref.dtype)


def _merge(um, attn_b, o_lat, gates, w_a, w_b, w_c, wuv_t, *, tm):
    T = um.shape[0]
    D = w_a.shape[1]
    row = lambda w: pl.BlockSpec((tm, w), lambda i: (i, 0))
    full = lambda a: pl.BlockSpec(a.shape, lambda i: (0,) * a.ndim)
    return pl.pallas_call(
        _merge_kernel,
        grid=(T // tm,),
        in_specs=[row(um.shape[1]), row(attn_b.shape[1]), row(o_lat.shape[1]),
                  pl.BlockSpec((tm, D), lambda i: (i, 0)),
                  pl.BlockSpec((tm, D), lambda i: (i, 1)),
                  pl.BlockSpec((tm, D), lambda i: (i, 2)),
                  full(w_a), full(w_b), full(w_c), full(wuv_t)],
        out_specs=pl.BlockSpec((tm, D), lambda i: (i, 0)),
        out_shape=jax.ShapeDtypeStruct((T, D), BF16),
        compiler_params=_cparams(("parallel",)),
        name="merge",
    )(um, attn_b, o_lat, gates, gates, gates, w_a, w_b, w_c, wuv_t)


def _out_proj_kernel(x_ref, w_ref, h_ref, g_ref, o_ref):
    o_ref[...] = h_ref[...] + _rms(_dot(x_ref[...], w_ref[...]), g_ref[...])


def _out_proj(x, w, h, post_g, *, tm):
    T, D = h.shape
    return pl.pallas_call(
        _out_proj_kernel,
        grid=(T // tm,),
        in_specs=[pl.BlockSpec((tm, x.shape[1]), lambda i: (i, 0)),
                  pl.BlockSpec(w.shape, lambda i: (0, 0)),
                  pl.BlockSpec((tm, D), lambda i: (i, 0)),
                  pl.BlockSpec((1, D), lambda i: (0, 0))],
        out_specs=pl.BlockSpec((tm, D), lambda i: (i, 0)),
        out_shape=jax.ShapeDtypeStruct((T, D), F32),
        compiler_params=_cparams(("parallel",)),
        name="out_proj",
    )(x, w, h, post_g)


def _rot_half_cols(w):
    half = MLA_ROPE // 2
    return jnp.concatenate([-w[..., half:], w[..., :half]], axis=-1)


def _pad_lanes(w):
    return jnp.pad(w, [(0, 0)] * (w.ndim - 1) + [(0, LANES - w.shape[-1])])


def _pack_mix_w_in(w):
    offs = np.cumsum([0, 2 * A_WIDTH, Q_W, KV_W, KV_W, MLA_Q_LORA, MLA_KV_LORA, MLA_ROPE])
    uv, q, k, v, cq, ckv, kr = [w[:, offs[n]:offs[n + 1]] for n in range(7)]
    gate = w[:, offs[7]:]
    small = jnp.concatenate([uv, q, cq, ckv, k, v, _pad_lanes(kr), _pad_lanes(_rot_half_cols(kr))], axis=1)
    assert small.shape[1] == N_SMALL
    return small.astype(BF16), gate.astype(BF16)


def _pack_w_uq(w):
    d = w.shape[0]
    w3 = w.reshape(d, MLA_HEADS, MLA_NOPE + MLA_ROPE)
    nope = w3[:, :, :MLA_NOPE].reshape(d, MLA_HEADS * MLA_NOPE)
    rope = w3[:, :, MLA_NOPE:]
    slot = lambda x: _pad_lanes(x).reshape(d, MLA_HEADS * LANES)
    return jnp.concatenate([nope, slot(rope), slot(_rot_half_cols(rope))], axis=1).astype(BF16)


def _rope_tables(pos):
    half = MLA_ROPE // 2
    inv = ROPE_BASE ** (-jnp.arange(half, dtype=F32) / half)
    ang = pos.astype(F32)[:, None] * inv
    cos, sin = jnp.cos(ang), jnp.sin(ang)
    return (_pad_lanes(jnp.concatenate([cos, cos], axis=1)),
            _pad_lanes(jnp.concatenate([sin, sin], axis=1)))


def kernel(x_prompt, x_sample, cache_moba_k, cache_moba_v, cache_mla_ckv, cache_mla_krope, page_table, ffn1_pre_g, ffn1_w_in, ffn1_w_out, ffn1_post_g, mix_pre_g, mix_w_in, mix_gate_b, sgu_ln_g, sgu_ln_b, sgu_w, sgu_b, sgu_proj, moba_proj, mla_q_norm_g, mla_w_uq, mla_kv_norm_g, mla_w_uk, mla_w_uv, mla_proj, mix_w_out, mix_post_g, ffn2_pre_g, ffn2_w_in, ffn2_w_out, ffn2_post_g):
    B, S, D = x_prompt.shape
    DB, dec_seq, _ = x_sample.shape
    depth = ffn1_w_in.shape[0]
    F = ffn1_w_out.shape[1]
    n_pool, page = cache_moba_k.shape[1:3]
    NP = page_table.shape[1]
    past = NP * page
    Tp = B * S
    T = Tp + DB
    assert dec_seq == 1 and DB % CHUNK == 0 and S % MOBA_BLOCK == 0 and past % MOBA_BLOCK == 0

    tm = _pick_tile(T, 640)
    tf = _pick_tile(F, 512)
    tn_small = _pick_tile(N_SMALL, 1280)
    tn_gate = _pick_tile(N_BRANCH * D, 2048)
    tm_merge = _pick_tile(T, 320)
    mla_tq = min(256, S)
    mla_tk = _pick_tile(S, 512)
    moba_tk = _pick_tile(S, 512, MOBA_BLOCK)

    cache_k_t = jnp.transpose(cache_moba_k, (0, 1, 3, 4, 2)).reshape(depth, n_pool, KV_W, page)
    cache_v_t = jnp.transpose(cache_moba_v, (0, 1, 3, 4, 2)).reshape(depth, n_pool, KV_W, page)
    cache_kr_t = jnp.swapaxes(cache_mla_krope, 2, 3)

    pos = jnp.concatenate([jnp.tile(jnp.arange(S, dtype=jnp.int32), B),
                           past + jnp.zeros((DB,), jnp.int32)])
    cos, sin = _rope_tables(pos)
    row = lambda a: a.reshape(1, -1)
    eye = jnp.eye(CHUNK, dtype=F32)
    tril = jnp.tril(jnp.ones((CHUNK, CHUNK), dtype=bool))
    pt_flat = page_table.astype(jnp.int32)

    h = jnp.concatenate([x_prompt.reshape(Tp, D), x_sample.reshape(DB, D)], axis=0)
    outs = [[] for _ in range(9)]
    for l in range(depth):
        h = _ffn_half(h, row(ffn1_pre_g[l]), ffn1_w_in[l].astype(BF16), ffn1_w_out[l].astype(BF16),
                      row(ffn1_post_g[l]), tm=tm, tf=tf)

        w_small, w_gate = _pack_mix_w_in(mix_w_in[l])
        pre_g = row(mix_pre_g[l])
        z = _norm_linear(h, pre_g, w_small, tm=tm, tn=tn_small, name="in_proj")
        gates = _norm_linear(h, pre_g, w_gate, row(mix_gate_b[l]), tm=tm, tn=tn_gate, name="gate_proj")

        w_s = sgu_w[l]
        mix_w = jnp.stack([jnp.where(tril, w_s, 0.0), w_s[:, 0, 0][:, None, None] * eye]).astype(BF16)
        b_s = sgu_b[l]
        mix_b = jnp.stack([jnp.repeat(b_s.T, A_GROUP_W, axis=1),
                           jnp.broadcast_to(jnp.repeat(b_s[:, 0], A_GROUP_W)[None], (CHUNK, A_WIDTH))])
        um, v_rows = _gmlp(z, row(sgu_ln_g[l]), row(sgu_ln_b[l]), mix_w, mix_b, n_prompt_chunks=Tp // CHUNK)

        wuk_t = jnp.transpose(mla_w_uk[l], (1, 2, 0)).astype(BF16)
        q_mla, ckv, krope, kc = _mla_prep(z, cos, sin, row(mla_q_norm_g[l]), row(mla_kv_norm_g[l]),
                                          _pack_w_uq(mla_w_uq[l]), wuk_t, tm=tm)

        q_aug, k_aug = _moba_select(z, B=B, S=S)
        v_t = z[:Tp, OFF_V:OFF_V + KV_W].T.reshape(MOBA_KV_HEADS, MOBA_HEAD_DIM, Tp).astype(BF16)
        attn_p = _flash(q_aug, k_aug, v_t, B=B, S=S, nh=MOBA_GROUP, tq=MOBA_BLOCK, tk=moba_tk,
                        qw=LANES, dv=MOBA_HEAD_DIM, name="moba_prompt")
        ckv_t = kc[:Tp, :MLA_KV_LORA].T[None]
        olat_p = _flash(q_mla, kc[None], ckv_t, B=B, S=S, nh=MLA_HEADS, tq=mla_tq, tk=mla_tk,
                        qw=MLA_QW, dv=MLA_KV_LORA, name="mla_prompt")

        zs = z[Tp:]
        q_s = zs[:, OFF_Q:OFF_Q + Q_W].reshape(DB, MOBA_HEADS, MOBA_HEAD_DIM)
        k_new = zs[:, OFF_K:OFF_K + KV_W]
        v_new = zs[:, OFF_V:OFF_V + KV_W]
        kvh = jnp.arange(MOBA_HEADS) // MOBA_GROUP
        lane_kvh = jnp.arange(KV_W) // MOBA_HEAD_DIM
        q_slot = jnp.where(lane_kvh[None, None, :] == kvh[None, :, None],
                           jnp.tile(q_s * MOBA_SCALE, (1, 1, MOBA_KV_HEADS)), 0.0)
        top = _moba_sample_pick(pt_flat, q_slot, cache_k_t, l)[:, :, :MOBA_TOPK]
        ppb = MOBA_BLOCK // page
        logical = (top[..., None] * ppb + jnp.arange(ppb, dtype=jnp.int32)).reshape(DB, -1)
        phys = jnp.take_along_axis(pt_flat, logical, axis=1).reshape(-1)
        o_slot = _moba_sample_attend(phys, q_slot, k_new.reshape(DB, 1, KV_W), v_new.reshape(DB, 1, KV_W),
                                     cache_k_t, cache_v_t, l)
        o_slot = o_slot.reshape(DB, MOBA_HEADS, MOBA_KV_HEADS, MOBA_HEAD_DIM)
        attn_s = jnp.concatenate([o_slot[:, hh, hh // MOBA_GROUP] for hh in range(MOBA_HEADS)], axis=1)
        kc_new = jnp.concatenate([ckv[Tp:], krope[Tp:]], axis=1).reshape(DB, 1, MLA_QW)
        olat_s = _mla_sample(pt_flat, q_mla[Tp:].reshape(DB, MLA_HEADS, MLA_QW), kc_new,
                             cache_mla_ckv, cache_kr_t, l)

        attn_b = jnp.concatenate([attn_p, attn_s.astype(BF16)], axis=0)
        o_lat = jnp.concatenate([olat_p, olat_s.reshape(DB, MLA_HEADS * MLA_KV_LORA)], axis=0)
        wuv_t = jnp.transpose(mla_w_uv[l], (1, 0, 2)).astype(BF16)
        merged = _merge(um, attn_b, o_lat, gates, sgu_proj[l].astype(BF16), moba_proj[l].astype(BF16),
                        mla_proj[l].astype(BF16), wuv_t, tm=tm_merge)
        h = _out_proj(merged, mix_w_out[l].astype(BF16), h, row(mix_post_g[l]), tm=tm)

        h = _ffn_half(h, row(ffn2_pre_g[l]), ffn2_w_in[l].astype(BF16), ffn2_w_out[l].astype(BF16),
                      row(ffn2_post_g[l]), tm=tm, tf=tf)

        k_all = z[:, OFF_K:OFF_K + KV_W]
        v_all = z[:, OFF_V:OFF_V + KV_W]
        kr_all = krope[:, :MLA_ROPE]
        outs[0].append(k_all[:Tp].reshape(B, S, MOBA_KV_HEADS, MOBA_HEAD_DIM))
        outs[1].append(v_all[:Tp].reshape(B, S, MOBA_KV_HEADS, MOBA_HEAD_DIM))
        outs[2].append(ckv[:Tp].reshape(B, S, MLA_KV_LORA))
        outs[3].append(kr_all[:Tp].reshape(B, S, MLA_ROPE))
        outs[4].append(k_all[Tp:].reshape(DB, 1, MOBA_KV_HEADS, MOBA_HEAD_DIM))
        outs[5].append(v_all[Tp:].reshape(DB, 1, MOBA_KV_HEADS, MOBA_HEAD_DIM))
        outs[6].append(ckv[Tp:].reshape(DB, 1, MLA_KV_LORA))
        outs[7].append(kr_all[Tp:].reshape(DB, 1, MLA_ROPE))
        outs[8].append(v_rows[Tp:].reshape(DB, 1, A_WIDTH))

    return (h[:Tp].reshape(B, S, D), h[Tp:].reshape(DB, 1, D)) + tuple(jnp.stack(o) for o in outs)
```

```python
import functools

import numpy as np
import jax
import jax.numpy as jnp
from jax import lax
from jax.experimental import pallas as pl
from jax.experimental.pallas import tpu as pltpu

F32 = jnp.float32
BF16 = jnp.bfloat16

CHUNK = 128
A_GROUPS = 4
A_GROUP_W = 128
A_WIDTH = A_GROUPS * A_GROUP_W
MOBA_HEADS = 8
MOBA_KV_HEADS = 2
MOBA_GROUP = MOBA_HEADS // MOBA_KV_HEADS
MOBA_HEAD_DIM = 64
MOBA_BLOCK = 256
MOBA_TOPK = 3
MLA_HEADS = 8
MLA_NOPE = 64
MLA_ROPE = 32
MLA_V = 64
MLA_Q_LORA = 384
MLA_KV_LORA = 128
MLA_SCALE = (MLA_NOPE + MLA_ROPE) ** -0.5
MOBA_SCALE = MOBA_HEAD_DIM ** -0.5
ROPE_BASE = 10000.0
N_BRANCH = 3
EPS = 1e-6
NEG = -1e30

LANES = 128
VMEM_LIMIT = 56 * 1024 * 1024
MASK_BIG = 2.0 ** 100
PICK_NEG = -3.0e38

KV_W = MOBA_KV_HEADS * MOBA_HEAD_DIM
Q_W = MOBA_HEADS * MOBA_HEAD_DIM
OFF_UV = 0
OFF_Q = OFF_UV + 2 * A_WIDTH
OFF_CQ = OFF_Q + Q_W
OFF_CKV = OFF_CQ + MLA_Q_LORA
OFF_K = OFF_CKV + MLA_KV_LORA
OFF_V = OFF_K + KV_W
OFF_KR = OFF_V + KV_W
OFF_KRROT = OFF_KR + LANES
N_SMALL = OFF_KRROT + LANES
MLA_QW = 2 * LANES
SAMPLE_PAGES_PER_STEP = 32


def _cparams(sem, vmem=VMEM_LIMIT):
    return pltpu.CompilerParams(dimension_semantics=sem, vmem_limit_bytes=vmem)


def _pick_tile(n, cap, mult=LANES):
    best = None
    t = mult
    while t <= min(n, cap):
        if n % t == 0:
            best = t
        t += mult
    assert best is not None, (n, cap, mult)
    return best


def _rms(x, g):
    ms = jnp.mean(x * x, axis=-1, keepdims=True)
    return x * lax.rsqrt(ms + EPS) * g


def _sigmoid(x):
    return 1.0 / (1.0 + jnp.exp(-x))


def _dot(a, b):
    return jnp.dot(a, b, preferred_element_type=F32)


def _dot_nt(a, b, precision=None):
    return lax.dot_general(a, b, (((1,), (1,)), ((), ())), precision=precision,
                           preferred_element_type=F32)


def _ffn_kernel(h_ref, pre_ref, wg_ref, wu_ref, wo_ref, post_ref, o_ref, xn_ref, acc_ref, *, nj):
    j = pl.program_id(1)

    @pl.when(j == 0)
    def _():
        xn_ref[...] = _rms(h_ref[...], pre_ref[...]).astype(BF16)
        acc_ref[...] = jnp.zeros_like(acc_ref)

    xn = xn_ref[...]
    gate = _dot(xn, wg_ref[...])
    up = _dot(xn, wu_ref[...])
    act = (gate * _sigmoid(gate) * up).astype(BF16)
    acc_ref[...] += _dot(act, wo_ref[...])

    @pl.when(j == nj - 1)
    def _():
        o_ref[...] = h_ref[...] + 0.5 * _rms(acc_ref[...], post_ref[...])


def _ffn_half(h, pre_g, w_in, w_out, post_g, *, tm, tf):
    T, D = h.shape
    F = w_out.shape[0]
    nj = F // tf
    return pl.pallas_call(
        functools.partial(_ffn_kernel, nj=nj),
        grid=(T // tm, nj),
        in_specs=[
            pl.BlockSpec((tm, D), lambda i, j: (i, 0)),
            pl.BlockSpec((1, D), lambda i, j: (0, 0)),
            pl.BlockSpec((D, tf), lambda i, j: (0, j)),
            pl.BlockSpec((D, tf), lambda i, j: (0, j + nj)),
            pl.BlockSpec((tf, D), lambda i, j: (j, 0)),
            pl.BlockSpec((1, D), lambda i, j: (0, 0)),
        ],
        out_specs=pl.BlockSpec((tm, D), lambda i, j: (i, 0)),
        out_shape=jax.ShapeDtypeStruct((T, D), F32),
        scratch_shapes=[pltpu.VMEM((tm, D), BF16), pltpu.VMEM((tm, D), F32)],
        compiler_params=_cparams(("parallel", "arbitrary")),
        name="ffn_half",
    )(h, pre_g, w_in, w_in, w_out, post_g)


def _norm_linear_kernel(x_ref, g_ref, w_ref, *rest, gate):
    if gate:
        b_ref, o_ref, xn_ref = rest
    else:
        o_ref, xn_ref = rest

    @pl.when(pl.program_id(1) == 0)
    def _():
        xn_ref[...] = _rms(x_ref[...], g_ref[...]).astype(BF16)

    y = _dot(xn_ref[...], w_ref[...])
    if gate:
        y = _sigmoid(y + b_ref[...])
    o_ref[...] = y.astype(o_ref.dtype)


def _norm_linear(x, g, w, bias=None, *, tm, tn, name):
    T, D = x.shape
    N = w.shape[1]
    in_specs = [
        pl.BlockSpec((tm, D), lambda i, j: (i, 0)),
        pl.BlockSpec((1, D), lambda i, j: (0, 0)),
        pl.BlockSpec((D, tn), lambda i, j: (0, j)),
    ]
    args = [x, g, w]
    if bias is not None:
        in_specs.append(pl.BlockSpec((1, tn), lambda i, j: (0, j)))
        args.append(bias)
    return pl.pallas_call(
        functools.partial(_norm_linear_kernel, gate=bias is not None),
        grid=(T // tm, N // tn),
        in_specs=in_specs,
        out_specs=pl.BlockSpec((tm, tn), lambda i, j: (i, j)),
        out_shape=jax.ShapeDtypeStruct((T, N), F32),
        scratch_shapes=[pltpu.VMEM((tm, D), BF16)],
        compiler_params=_cparams(("parallel", "arbitrary")),
        name=name,
    )(*args)


def _gelu_tanh(x):
    c = np.sqrt(2.0 / np.pi).astype(np.float32)
    return 0.5 * x * (1.0 + jnp.tanh(c * (x + 0.044715 * (x * x * x))))


def _gmlp_kernel(z_ref, lng_ref, lnb_ref, w_ref, b_ref, um_ref, v_ref):
    a = _gelu_tanh(z_ref[...])
    u = a[:, :A_WIDTH]
    v = a[:, A_WIDTH:]
    mu = jnp.mean(v, axis=-1, keepdims=True)
    vc = v - mu
    var = jnp.mean(vc * vc, axis=-1, keepdims=True)
    vn = vc * lax.rsqrt(var + EPS) * lng_ref[...] + lnb_ref[...]
    v_ref[...] = vn
    vb = vn.astype(BF16)
    mixed = jnp.concatenate(
        [_dot(w_ref[0, g], vb[:, g * A_GROUP_W:(g + 1) * A_GROUP_W]) for g in range(A_GROUPS)], axis=1)
    um_ref[...] = (u * (mixed + b_ref[0])).astype(BF16)


def _gmlp(z_small, ln_g, ln_b, mix_w, mix_b, *, n_prompt_chunks):
    T = z_small.shape[0]
    n_chunks = T // CHUNK
    mode = lambda c: c // n_prompt_chunks
    return pl.pallas_call(
        _gmlp_kernel,
        grid=(n_chunks,),
        in_specs=[
            pl.BlockSpec((CHUNK, 2 * A_WIDTH), lambda c: (c, OFF_UV // (2 * A_WIDTH))),
            pl.BlockSpec((1, A_WIDTH), lambda c: (0, 0)),
            pl.BlockSpec((1, A_WIDTH), lambda c: (0, 0)),
            pl.BlockSpec((1, A_GROUPS, CHUNK, CHUNK), lambda c: (mode(c), 0, 0, 0)),
            pl.BlockSpec((1, CHUNK, A_WIDTH), lambda c: (mode(c), 0, 0)),
        ],
        out_specs=[pl.BlockSpec((CHUNK, A_WIDTH), lambda c: (c, 0)),
                   pl.BlockSpec((CHUNK, A_WIDTH), lambda c: (c, 0))],
        out_shape=[jax.ShapeDtypeStruct((T, A_WIDTH), BF16),
                   jax.ShapeDtypeStruct((T, A_WIDTH), F32)],
        compiler_params=_cparams(("parallel",)),
        name="gmlp_chunk",
    )(z_small, ln_g, ln_b, mix_w, mix_b)


def _mla_prep_kernel(zc_ref, zk_ref, cos_ref, sin_ref, qg_ref, kvg_ref, wuq_ref, wuk_ref,
                     q_ref, ckv_ref, kr_ref, kc_ref):
    zc = zc_ref[...]
    cqn = _rms(zc[:, :MLA_Q_LORA], qg_ref[...]).astype(BF16)
    q = _dot(cqn, wuq_ref[...])
    nope_w = MLA_HEADS * MLA_NOPE
    slot_w = MLA_HEADS * LANES
    cos = cos_ref[...]
    sin = sin_ref[...]
    cos_h = jnp.concatenate([cos] * MLA_HEADS, axis=1)
    sin_h = jnp.concatenate([sin] * MLA_HEADS, axis=1)
    q_rope = q[:, nope_w:nope_w + slot_w] * cos_h + q[:, nope_w + slot_w:] * sin_h
    for h in range(MLA_HEADS):
        q_nope = q[:, h * MLA_NOPE:(h + 1) * MLA_NOPE].astype(BF16)
        q_lat = _dot(q_nope, wuk_ref[h])
        q_ref[:, h * MLA_QW:h * MLA_QW + LANES] = (q_lat * MLA_SCALE).astype(BF16)
        q_ref[:, h * MLA_QW + LANES:(h + 1) * MLA_QW] = (
            q_rope[:, h * LANES:(h + 1) * LANES] * MLA_SCALE).astype(BF16)
    ckv = _rms(zc[:, MLA_Q_LORA:], kvg_ref[...])
    zk = zk_ref[...]
    kr = zk[:, :LANES] * cos + zk[:, LANES:] * sin
    ckv_ref[...] = ckv
    kr_ref[...] = kr
    kc_ref[:, :LANES] = ckv.astype(BF16)
    kc_ref[:, LANES:] = kr.astype(BF16)


def _mla_prep(z_small, cos, sin, q_norm_g, kv_norm_g, wuq, wuk_t, *, tm):
    T = z_small.shape[0]
    cw = MLA_Q_LORA + MLA_KV_LORA
    return pl.pallas_call(
        _mla_prep_kernel,
        grid=(T // tm,),
        in_specs=[
            pl.BlockSpec((tm, cw), lambda i: (i, OFF_CQ // cw)),
            pl.BlockSpec((tm, 2 * LANES), lambda i: (i, OFF_KR // (2 * LANES))),
            pl.BlockSpec((tm, LANES), lambda i: (i, 0)),
            pl.BlockSpec((tm, LANES), lambda i: (i, 0)),
            pl.BlockSpec((1, MLA_Q_LORA), lambda i: (0, 0)),
            pl.BlockSpec((1, MLA_KV_LORA), lambda i: (0, 0)),
            pl.BlockSpec(wuq.shape, lambda i: (0, 0)),
            pl.BlockSpec(wuk_t.shape, lambda i: (0, 0, 0)),
        ],
        out_specs=[
            pl.BlockSpec((tm, MLA_HEADS * MLA_QW), lambda i: (i, 0)),
            pl.BlockSpec((tm, LANES), lambda i: (i, 0)),
            pl.BlockSpec((tm, LANES), lambda i: (i, 0)),
            pl.BlockSpec((tm, 2 * LANES), lambda i: (i, 0)),
        ],
        out_shape=[
            jax.ShapeDtypeStruct((T, MLA_HEADS * MLA_QW), BF16),
            jax.ShapeDtypeStruct((T, LANES), F32),
            jax.ShapeDtypeStruct((T, LANES), F32),
            jax.ShapeDtypeStruct((T, 2 * LANES), BF16),
        ],
        compiler_params=_cparams(("parallel",)),
        name="mla_prep",
    )(z_small, z_small, cos, sin, q_norm_g, kv_norm_g, wuq, wuk_t)


def _topk_lanes(sc, lane_f, k):
    picks = []
    for _ in range(k):
        m = jnp.max(sc, axis=1, keepdims=True)
        idx = jnp.min(jnp.where(sc == m, lane_f, 1e9), axis=1, keepdims=True)
        hit = lane_f == idx
        picks.append((idx, m > 0.5 * PICK_NEG, hit))
        sc = jnp.where(hit, PICK_NEG, sc)
    return picks


def _moba_select_kernel(q_ref, k_ref, qa_ref, ka_ref, means_ref, *, nb):
    i = pl.program_id(1)
    hd = MOBA_HEAD_DIM
    tq = q_ref.shape[0]

    @pl.when(i == 0)
    def _():
        means_ref[...] = jnp.zeros_like(means_ref)

    q = q_ref[...]
    k = k_ref[...]
    means = means_ref[0:nb, :]
    blk = lax.broadcasted_iota(jnp.int32, (tq, nb), 1)
    blk_f = blk.astype(F32)
    past = blk < i
    pad = jnp.zeros((tq, LANES - hd - nb), F32)
    pieces = []
    for h in range(MOBA_HEADS):
        g = h // MOBA_GROUP
        qh = q[:, h * hd:(h + 1) * hd]
        sc = _dot_nt(qh, means[:, g * hd:(g + 1) * hd], precision=lax.Precision.HIGHEST)
        sc = jnp.where(past, sc, PICK_NEG)
        chosen = blk == i
        for _, ok, hit in _topk_lanes(sc, blk_f, min(MOBA_TOPK, nb)):
            chosen = chosen | (hit & ok)
        pieces += [qh * MOBA_SCALE, jnp.where(chosen, 0.0, 1.0), pad]
    qa_ref[...] = jnp.concatenate(pieces, axis=1).astype(BF16)
    onehot = jnp.where(blk == i, -MASK_BIG, 0.0)
    for g in range(MOBA_KV_HEADS):
        ka_ref[g] = jnp.concatenate([k[:, g * hd:(g + 1) * hd], onehot, pad], axis=1).astype(BF16)
    means_ref[pl.ds(i, 1), :] = jnp.sum(k, axis=0, keepdims=True) * (1.0 / MOBA_BLOCK)


def _moba_select(z_small, *, B, S):
    nb = S // MOBA_BLOCK
    Tp = B * S
    tq = MOBA_BLOCK
    assert MOBA_HEAD_DIM + nb <= LANES
    return pl.pallas_call(
        functools.partial(_moba_select_kernel, nb=nb),
        grid=(B, nb),
        in_specs=[
            pl.BlockSpec((tq, Q_W), lambda b, i: (b * nb + i, OFF_Q // Q_W)),
            pl.BlockSpec((tq, KV_W), lambda b, i: (b * nb + i, OFF_K // KV_W)),
        ],
        out_specs=[
            pl.BlockSpec((tq, MOBA_HEADS * LANES), lambda b, i: (b * nb + i, 0)),
            pl.BlockSpec((MOBA_KV_HEADS, tq, LANES), lambda b, i: (0, b * nb + i, 0)),
        ],
        out_shape=[
            jax.ShapeDtypeStruct((Tp, MOBA_HEADS * LANES), BF16),
            jax.ShapeDtypeStruct((MOBA_KV_HEADS, Tp, LANES), BF16),
        ],
        scratch_shapes=[pltpu.VMEM((max(nb, 8), KV_W), F32)],
        compiler_params=_cparams(("arbitrary", "arbitrary")),
        name="moba_select",
    )(z_small, z_small)


def _flash_kernel(q_ref, k_ref, vt_ref, o_ref, qt_ref, m_ref, l_ref, acc_ref, *, nh, tq, tk, qw, dv):
    qi = pl.program_id(2)
    kj = pl.program_id(3)
    last = (qi * tq + (tq - 1)) // tk
    cols = nh * tq

    @pl.when(kj == 0)
    def _():
        for h in range(nh):
            qt_ref[:, h * tq:(h + 1) * tq] = q_ref[:, h * qw:(h + 1) * qw].astype(F32).T.astype(BF16)
        m_ref[...] = jnp.full_like(m_ref, -jnp.inf)
        l_ref[...] = jnp.zeros_like(l_ref)
        acc_ref[...] = jnp.zeros_like(acc_ref)

    def step(masked):
        s = _dot(k_ref[0], qt_ref[...])
        if masked:
            kpos = kj * tk + lax.broadcasted_iota(jnp.int32, (tk, cols), 0)
            qpos = qi * tq + (lax.broadcasted_iota(jnp.int32, (tk, cols), 1) & (tq - 1))
            s = jnp.where(kpos <= qpos, s, NEG)
        m_prev = m_ref[...]
        m_new = jnp.maximum(m_prev, jnp.max(s, axis=0, keepdims=True))
        alpha = jnp.exp(m_prev - m_new)
        p = jnp.exp(s - m_new)
        l_ref[...] = alpha * l_ref[...] + jnp.sum(p, axis=0, keepdims=True)
        acc_ref[...] = alpha * acc_ref[...] + _dot(vt_ref[0], p.astype(BF16))
        m_ref[...] = m_new

    @pl.when(kj < last)
    def _():
        step(False)

    @pl.when(kj == last)
    def _():
        step(True)
        out_t = acc_ref[...] / l_ref[...]
        stacked = jnp.concatenate([out_t[:, h * tq:(h + 1) * tq] for h in range(nh)], axis=0)
        o_ref[...] = stacked.T.astype(o_ref.dtype)


def _flash(q, k, vt, *, B, S, nh, tq, tk, qw, dv, name):
    G = k.shape[0]
    nq, nk = S // tq, S // tk
    assert tq & (tq - 1) == 0 and S % tq == 0 and S % tk == 0
    last_k = lambda b, qi, kj: b * nk + jnp.minimum(kj, (qi * tq + (tq - 1)) // tk)
    return pl.pallas_call(
        functools.partial(_flash_kernel, nh=nh, tq=tq, tk=tk, qw=qw, dv=dv),
        grid=(B, G, nq, nk),
        in_specs=[pl.BlockSpec((tq, nh * qw), lambda b, g, qi, kj: (b * nq + qi, g)),
                  pl.BlockSpec((1, tk, qw), lambda b, g, qi, kj: (g, last_k(b, qi, kj), 0)),
                  pl.BlockSpec((1, dv, tk), lambda b, g, qi, kj: (g, 0, last_k(b, qi, kj)))],
        out_specs=pl.BlockSpec((tq, nh * dv), lambda b, g, qi, kj: (b * nq + qi, g)),
        out_shape=jax.ShapeDtypeStruct((B * S, G * nh * dv), BF16),
        scratch_shapes=[pltpu.VMEM((qw, nh * tq), BF16), pltpu.VMEM((1, nh * tq), F32),
                        pltpu.VMEM((1, nh * tq), F32), pltpu.VMEM((dv, nh * tq), F32)],
        compiler_params=_cparams(("parallel", "parallel", "parallel", "arbitrary")),
        name=name,
    )(q, k, vt)


def _paged_chunks(page_id, pp, nc, page_copies, compute, carry):
    assert nc % 2 == 0
    seq = pl.program_id(0)
    n_seq = pl.num_programs(0)

    def copies(sq, c):
        return [page_copies(page_id(sq, c * pp + r), r, c % 2) for r in range(pp)]

    def start(sq, c):
        for cps in copies(sq, c):
            for cp in cps:
                cp.start()

    @pl.when(seq == 0)
    def _():
        start(seq, 0)

    for c in range(nc):
        if c + 1 < nc:
            start(seq, c + 1)
        else:
            @pl.when(seq + 1 < n_seq)
            def _():
                start(seq + 1, 0)
        for cps in copies(seq, c):
            for cp in cps:
                cp.wait()
        carry = compute(c % 2, c, carry)
    return carry


def _mla_sample_kernel(pt_ref, q_ref, kn_ref, ckv_hbm, kr_hbm, o_ref, ckv_buf, kr_buf, sem, *, layer, pp, nc):
    page = CHUNK

    def page_copies(pg, r, slot):
        return [pltpu.make_async_copy(ckv_hbm.at[layer, pg], ckv_buf.at[slot, pl.ds(r * page, page), :],
                                      sem.at[slot, 0]),
                pltpu.make_async_copy(kr_hbm.at[layer, pg], kr_buf.at[slot, :, pl.ds(r * page, page)],
                                      sem.at[slot, 1])]

    q = q_ref[0]
    q_lat = q[:, :MLA_KV_LORA]
    q_rope = q[:, LANES:LANES + MLA_ROPE]
    kn = kn_ref[0]

    def compute(slot, c, carry):
        m_prev, l_prev, acc_prev = carry
        ckv = ckv_buf[slot].astype(BF16)
        kr_t = kr_buf[slot].astype(BF16)
        s = _dot_nt(q_lat, ckv) + _dot(q_rope, kr_t)
        m_new = jnp.maximum(m_prev, jnp.max(s, axis=1, keepdims=True))
        alpha = jnp.exp(m_prev - m_new)
        p = jnp.exp(s - m_new)
        l_new = alpha * l_prev + jnp.sum(p, axis=1, keepdims=True)
        return m_new, l_new, alpha * acc_prev + _dot(p.astype(BF16), ckv)

    init = (jnp.sum(q.astype(F32) * kn, axis=1, keepdims=True),
            jnp.ones((MLA_HEADS, 1), F32),
            jnp.broadcast_to(kn[:, :MLA_KV_LORA], (MLA_HEADS, MLA_KV_LORA)))
    _, l, acc = _paged_chunks(lambda sq, n: pt_ref[sq, n], pp, nc, page_copies, compute, init)
    o_ref[0] = (acc / l).astype(o_ref.dtype)


def _mla_sample(page_table, q_s, kc_new, cache_ckv, cache_kr_t, layer):
    DB, NP = page_table.shape
    page = cache_ckv.shape[2]
    assert page == CHUNK
    pp = SAMPLE_PAGES_PER_STEP
    assert NP % (2 * pp) == 0
    grid_spec = pltpu.PrefetchScalarGridSpec(
        num_scalar_prefetch=1,
        grid=(DB,),
        in_specs=[pl.BlockSpec((1, MLA_HEADS, MLA_QW), lambda b, pt: (b, 0, 0)),
                  pl.BlockSpec((1, 1, MLA_QW), lambda b, pt: (b, 0, 0)),
                  pl.BlockSpec(memory_space=pl.ANY),
                  pl.BlockSpec(memory_space=pl.ANY)],
        out_specs=pl.BlockSpec((1, MLA_HEADS, MLA_KV_LORA), lambda b, pt: (b, 0, 0)),
        scratch_shapes=[pltpu.VMEM((2, pp * page, MLA_KV_LORA), F32),
                        pltpu.VMEM((2, MLA_ROPE, pp * page), F32),
                        pltpu.SemaphoreType.DMA((2, 2))],
    )
    return pl.pallas_call(
        functools.partial(_mla_sample_kernel, layer=layer, pp=pp, nc=NP // pp),
        grid_spec=grid_spec,
        out_shape=jax.ShapeDtypeStruct((DB, MLA_HEADS, MLA_KV_LORA), BF16),
        compiler_params=_cparams(("arbitrary",)),
        name="mla_sample",
    )(page_table, q_s, kc_new, cache_ckv, cache_kr_t)


def _moba_sample_pick_kernel(pt_ref, q_ref, k_hbm, top_ref, k_buf, sem, *, layer, pp, nc):
    ppb = MOBA_BLOCK // CHUNK
    bpc = pp // ppb
    nb = nc * bpc
    col = lax.broadcasted_iota(jnp.int32, (KV_W, nb), 1)

    def page_copies(pg, r, slot):
        return [pltpu.make_async_copy(k_hbm.at[layer, pg], k_buf.at[slot, r], sem.at[slot])]

    def compute(slot, c, sums):
        for n in range(bpc):
            tot = k_buf[slot, n * ppb]
            for r in range(1, ppb):
                tot = tot + k_buf[slot, n * ppb + r]
            sums = jnp.where(col == c * bpc + n, jnp.sum(tot, axis=1, keepdims=True), sums)
        return sums

    sums = _paged_chunks(lambda sq, n: pt_ref[sq, n], pp, nc, page_copies, compute,
                         jnp.zeros((KV_W, nb), F32))
    means_t = sums * (1.0 / MOBA_BLOCK)
    sc = jnp.dot(q_ref[0], means_t, precision=lax.Precision.HIGHEST, preferred_element_type=F32)
    lane_f = lax.broadcasted_iota(jnp.int32, (MOBA_HEADS, nb), 1).astype(F32)
    picks = _topk_lanes(sc, lane_f, MOBA_TOPK)
    out_lane = lax.broadcasted_iota(jnp.int32, (MOBA_HEADS, LANES), 1)
    out = jnp.zeros((MOBA_HEADS, LANES), F32)
    for t, (idx, _, _) in enumerate(picks):
        out = jnp.where(out_lane == t, idx, out)
    top_ref[0] = out.astype(jnp.int32)


def _moba_sample_pick(page_table, q_slot, cache_k_t, layer):
    DB, NP = page_table.shape
    page = cache_k_t.shape[3]
    pp = SAMPLE_PAGES_PER_STEP
    ppb = MOBA_BLOCK // page
    assert page == CHUNK and NP % (2 * pp) == 0 and pp % ppb == 0
    assert NP // ppb >= MOBA_TOPK
    grid_spec = pltpu.PrefetchScalarGridSpec(
        num_scalar_prefetch=1,
        grid=(DB,),
        in_specs=[pl.BlockSpec((1, MOBA_HEADS, KV_W), lambda b, pt: (b, 0, 0)),
                  pl.BlockSpec(memory_space=pl.ANY)],
        out_specs=pl.BlockSpec((1, MOBA_HEADS, LANES), lambda b, pt: (b, 0, 0)),
        scratch_shapes=[pltpu.VMEM((2, pp, KV_W, page), F32), pltpu.SemaphoreType.DMA((2,))],
    )
    return pl.pallas_call(
        functools.partial(_moba_sample_pick_kernel, layer=layer, pp=pp, nc=NP // pp),
        grid_spec=grid_spec,
        out_shape=jax.ShapeDtypeStruct((DB, MOBA_HEADS, LANES), jnp.int32),
        compiler_params=_cparams(("arbitrary",)),
        name="moba_sample_pick",
    )(page_table, q_slot, cache_k_t)


def _moba_sample_attend_kernel(ph_ref, q_ref, kn_ref, vn_ref, k_hbm, v_hbm, o_ref, k_buf, v_buf, sem,
                               *, layer, npg, nc):
    nh = MOBA_HEADS
    hpc = nh // nc
    pp = hpc * npg

    def page_copies(pg, r, slot):
        return [pltpu.make_async_copy(k_hbm.at[layer, pg], k_buf.at[slot, r], sem.at[slot, 0]),
                pltpu.make_async_copy(v_hbm.at[layer, pg], v_buf.at[slot, r], sem.at[slot, 1])]

    q = q_ref[0]
    qb = q.astype(BF16)
    s_own = jnp.sum(q * kn_ref[0], axis=1, keepdims=True)
    v_own = vn_ref[0]
    row = lax.broadcasted_iota(jnp.int32, (nh, npg * CHUNK), 0)
    orow = lax.broadcasted_iota(jnp.int32, (nh, KV_W), 0)

    def compute(slot, c, out):
        s = jnp.zeros((nh, npg * CHUNK), F32)
        for j in range(hpc):
            k_t = jnp.concatenate([k_buf[slot, j * npg + r].astype(BF16) for r in range(npg)], axis=1)
            s = jnp.where(row == c * hpc + j, _dot(qb, k_t), s)
        m = jnp.maximum(jnp.max(s, axis=1, keepdims=True), s_own)
        p = jnp.exp(s - m)
        p_own = jnp.exp(s_own - m)
        l = jnp.sum(p, axis=1, keepdims=True) + p_own
        pb = p.astype(BF16)
        acc = p_own * v_own
        for j in range(hpc):
            v_t = jnp.concatenate([v_buf[slot, j * npg + r].astype(BF16) for r in range(npg)], axis=1)
            acc = acc + jnp.where(orow == c * hpc + j, _dot_nt(pb, v_t), 0.0)
        mine = (orow >= c * hpc) & (orow < (c + 1) * hpc)
        return jnp.where(mine, acc / l, out)

    page_id = lambda sq, n: ph_ref[sq * (nh * npg) + n]
    o_ref[0] = _paged_chunks(page_id, pp, nc, page_copies, compute, jnp.zeros((nh, KV_W), F32))


def _moba_sample_attend(phys, q_slot, k_new, v_new, cache_k_t, cache_v_t, layer):
    DB, H = q_slot.shape[:2]
    page = cache_k_t.shape[3]
    npg = MOBA_TOPK * (MOBA_BLOCK // page)
    nc = 2
    pp = (H // nc) * npg
    grid_spec = pltpu.PrefetchScalarGridSpec(
        num_scalar_prefetch=1,
        grid=(DB,),
        in_specs=[pl.BlockSpec((1, H, KV_W), lambda b, ph: (b, 0, 0)),
                  pl.BlockSpec((1, 1, KV_W), lambda b, ph: (b, 0, 0)),
                  pl.BlockSpec((1, 1, KV_W), lambda b, ph: (b, 0, 0)),
                  pl.BlockSpec(memory_space=pl.ANY),
                  pl.BlockSpec(memory_space=pl.ANY)],
        out_specs=pl.BlockSpec((1, H, KV_W), lambda b, ph: (b, 0, 0)),
        scratch_shapes=[pltpu.VMEM((2, pp, KV_W, page), F32), pltpu.VMEM((2, pp, KV_W, page), F32),
                        pltpu.SemaphoreType.DMA((2, 2))],
    )
    return pl.pallas_call(
        functools.partial(_moba_sample_attend_kernel, layer=layer, npg=npg, nc=nc),
        grid_spec=grid_spec,
        out_shape=jax.ShapeDtypeStruct((DB, H, KV_W), F32),
        compiler_params=_cparams(("arbitrary",)),
        name="moba_sample_attend",
    )(phys, q_slot, k_new, v_new, cache_k_t, cache_v_t)


def _merge_kernel(um_ref, ab_ref, ol_ref, g0_ref, g1_ref, g2_ref, wa_ref, wb_ref, wc_ref, wuv_ref, o_ref):
    y_a = _dot(um_ref[...], wa_ref[...])
    y_b = _dot(ab_ref[...], wb_ref[...])
    ol = ol_ref[...]
    oc = jnp.concatenate(
        [_dot(ol[:, h * MLA_KV_LORA:(h + 1) * MLA_KV_LORA], wuv_ref[h]) for h in range(MLA_HEADS)],
        axis=1).astype(BF16)
    y_c = _dot(oc, wc_ref[...])
    o_ref[...] = (g0_ref[...] * y_a + g1_ref[...] * y_b + g2_ref[...] * y_c).astype(o_ref.dtype)


def _merge(um, attn_b, o_lat, gates, w_a, w_b, w_c, wuv_t, *, tm):
    T = um.shape[0]
    D = w_a.shape[1]
    row = lambda w: pl.BlockSpec((tm, w), lambda i: (i, 0))
    full = lambda a: pl.BlockSpec(a.shape, lambda i: (0,) * a.ndim)
    return pl.pallas_call(
        _merge_kernel,
        grid=(T // tm,),
        in_specs=[row(um.shape[1]), row(attn_b.shape[1]), row(o_lat.shape[1]),
                  pl.BlockSpec((tm, D), lambda i: (i, 0)),
                  pl.BlockSpec((tm, D), lambda i: (i, 1)),
                  pl.BlockSpec((tm, D), lambda i: (i, 2)),
                  full(w_a), full(w_b), full(w_c), full(wuv_t)],
        out_specs=pl.BlockSpec((tm, D), lambda i: (i, 0)),
        out_shape=jax.ShapeDtypeStruct((T, D), BF16),
        compiler_params=_cparams(("parallel",)),
        name="merge",
    )(um, attn_b, o_lat, gates, gates, gates, w_a, w_b, w_c, wuv_t)


def _out_proj_kernel(x_ref, w_ref, h_ref, g_ref, o_ref):
    o_ref[...] = h_ref[...] + _rms(_dot(x_ref[...], w_ref[...]), g_ref[...])


def _out_proj(x, w, h, post_g, *, tm):
    T, D = h.shape
    return pl.pallas_call(
        _out_proj_kernel,
        grid=(T // tm,),
        in_specs=[pl.BlockSpec((tm, x.shape[1]), lambda i: (i, 0)),
                  pl.BlockSpec(w.shape, lambda i: (0, 0)),
                  pl.BlockSpec((tm, D), lambda i: (i, 0)),
                  pl.BlockSpec((1, D), lambda i: (0, 0))],
        out_specs=pl.BlockSpec((tm, D), lambda i: (i, 0)),
        out_shape=jax.ShapeDtypeStruct((T, D), F32),
        compiler_params=_cparams(("parallel",)),
        name="out_proj",
    )(x, w, h, post_g)


def _rot_half_cols(w):
    half = MLA_ROPE // 2
    return jnp.concatenate([-w[..., half:], w[..., :half]], axis=-1)


def _pad_lanes(w):
    return jnp.pad(w, [(0, 0)] * (w.ndim - 1) + [(0, LANES - w.shape[-1])])


def _pack_mix_w_in(w):
    offs = np.cumsum([0, 2 * A_WIDTH, Q_W, KV_W, KV_W, MLA_Q_LORA, MLA_KV_LORA, MLA_ROPE])
    uv, q, k, v, cq, ckv, kr = [w[:, offs[n]:offs[n + 1]] for n in range(7)]
    gate = w[:, offs[7]:]
    small = jnp.concatenate([uv, q, cq, ckv, k, v, _pad_lanes(kr), _pad_lanes(_rot_half_cols(kr))], axis=1)
    assert small.shape[1] == N_SMALL
    return small.astype(BF16), gate.astype(BF16)


def _pack_w_uq(w):
    d = w.shape[0]
    w3 = w.reshape(d, MLA_HEADS, MLA_NOPE + MLA_ROPE)
    nope = w3[:, :, :MLA_NOPE].reshape(d, MLA_HEADS * MLA_NOPE)
    rope = w3[:, :, MLA_NOPE:]
    slot = lambda x: _pad_lanes(x).reshape(d, MLA_HEADS * LANES)
    return jnp.concatenate([nope, slot(rope), slot(_rot_half_cols(rope))], axis=1).astype(BF16)


def _rope_tables(pos):
    half = MLA_ROPE // 2
    inv = ROPE_BASE ** (-jnp.arange(half, dtype=F32) / half)
    ang = pos.astype(F32)[:, None] * inv
    cos, sin = jnp.cos(ang), jnp.sin(ang)
    return (_pad_lanes(jnp.concatenate([cos, cos], axis=1)),
            _pad_lanes(jnp.concatenate([sin, sin], axis=1)))


def kernel(x_prompt, x_sample, cache_moba_k, cache_moba_v, cache_mla_ckv, cache_mla_krope, page_table, ffn1_pre_g, ffn1_w_in, ffn1_w_out, ffn1_post_g, mix_pre_g, mix_w_in, mix_gate_b, sgu_ln_g, sgu_ln_b, sgu_w, sgu_b, sgu_proj, moba_proj, mla_q_norm_g, mla_w_uq, mla_kv_norm_g, mla_w_uk, mla_w_uv, mla_proj, mix_w_out, mix_post_g, ffn2_pre_g, ffn2_w_in, ffn2_w_out, ffn2_post_g):
    B, S, D = x_prompt.shape
    DB, dec_seq, _ = x_sample.shape
    depth = ffn1_w_in.shape[0]
    F = ffn1_w_out.shape[1]
    n_pool, page = cache_moba_k.shape[1:3]
    NP = page_table.shape[1]
    past = NP * page
    Tp = B * S
    T = Tp + DB
    assert dec_seq == 1 and DB % CHUNK == 0 and S % MOBA_BLOCK == 0 and past % MOBA_BLOCK == 0

    tm = _pick_tile(T, 640)
    tf = _pick_tile(F, 512)
    tn_small = _pick_tile(N_SMALL, 1280)
    tn_gate = _pick_tile(N_BRANCH * D, 2048)
    tm_merge = _pick_tile(T, 320)
    mla_tq = min(256, S)
    mla_tk = _pick_tile(S, 512)
    moba_tk = _pick_tile(S, 512, MOBA_BLOCK)

    cache_k_t = jnp.transpose(cache_moba_k, (0, 1, 3, 4, 2)).reshape(depth, n_pool, KV_W, page)
    cache_v_t = jnp.transpose(cache_moba_v, (0, 1, 3, 4, 2)).reshape(depth, n_pool, KV_W, page)
    cache_kr_t = jnp.swapaxes(cache_mla_krope, 2, 3)

    pos = jnp.concatenate([jnp.tile(jnp.arange(S, dtype=jnp.int32), B),
                           past + jnp.zeros((DB,), jnp.int32)])
    cos, sin = _rope_tables(pos)
    row = lambda a: a.reshape(1, -1)
    eye = jnp.eye(CHUNK, dtype=F32)
    tril = jnp.tril(jnp.ones((CHUNK, CHUNK), dtype=bool))
    pt_flat = page_table.astype(jnp.int32)

    h = jnp.concatenate([x_prompt.reshape(Tp, D), x_sample.reshape(DB, D)], axis=0)
    outs = [[] for _ in range(9)]
    for l in range(depth):
        h = _ffn_half(h, row(ffn1_pre_g[l]), ffn1_w_in[l].astype(BF16), ffn1_w_out[l].astype(BF16),
                      row(ffn1_post_g[l]), tm=tm, tf=tf)

        w_small, w_gate = _pack_mix_w_in(mix_w_in[l])
        pre_g = row(mix_pre_g[l])
        z = _norm_linear(h, pre_g, w_small, tm=tm, tn=tn_small, name="in_proj")
        gates = _norm_linear(h, pre_g, w_gate, row(mix_gate_b[l]), tm=tm, tn=tn_gate, name="gate_proj")

        w_s = sgu_w[l]
        mix_w = jnp.stack([jnp.where(tril, w_s, 0.0), w_s[:, 0, 0][:, None, None] * eye]).astype(BF16)
        b_s = sgu_b[l]
        mix_b = jnp.stack([jnp.repeat(b_s.T, A_GROUP_W, axis=1),
                           jnp.broadcast_to(jnp.repeat(b_s[:, 0], A_GROUP_W)[None], (CHUNK, A_WIDTH))])
        um, v_rows = _gmlp(z, row(sgu_ln_g[l]), row(sgu_ln_b[l]), mix_w, mix_b, n_prompt_chunks=Tp // CHUNK)

        wuk_t = jnp.transpose(mla_w_uk[l], (1, 2, 0)).astype(BF16)
        q_mla, ckv, krope, kc = _mla_prep(z, cos, sin, row(mla_q_norm_g[l]), row(mla_kv_norm_g[l]),
                                          _pack_w_uq(mla_w_uq[l]), wuk_t, tm=tm)

        q_aug, k_aug = _moba_select(z, B=B, S=S)
        v_t = z[:Tp, OFF_V:OFF_V + KV_W].T.reshape(MOBA_KV_HEADS, MOBA_HEAD_DIM, Tp).astype(BF16)
        attn_p = _flash(q_aug, k_aug, v_t, B=B, S=S, nh=MOBA_GROUP, tq=MOBA_BLOCK, tk=moba_tk,
                        qw=LANES, dv=MOBA_HEAD_DIM, name="moba_prompt")
        ckv_t = kc[:Tp, :MLA_KV_LORA].T[None]
        olat_p = _flash(q_mla, kc[None], ckv_t, B=B, S=S, nh=MLA_HEADS, tq=mla_tq, tk=mla_tk,
                        qw=MLA_QW, dv=MLA_KV_LORA, name="mla_prompt")

        zs = z[Tp:]
        q_s = zs[:, OFF_Q:OFF_Q + Q_W].reshape(DB, MOBA_HEADS, MOBA_HEAD_DIM)
        k_new = zs[:, OFF_K:OFF_K + KV_W]
        v_new = zs[:, OFF_V:OFF_V + KV_W]
        kvh = jnp.arange(MOBA_HEADS) // MOBA_GROUP
        lane_kvh = jnp.arange(KV_W) // MOBA_HEAD_DIM
        q_slot = jnp.where(lane_kvh[None, None, :] == kvh[None, :, None],
                           jnp.tile(q_s * MOBA_SCALE, (1, 1, MOBA_KV_HEADS)), 0.0)
        top = _moba_sample_pick(pt_flat, q_slot, cache_k_t, l)[:, :, :MOBA_TOPK]
        ppb = MOBA_BLOCK // page
        logical = (top[..., None] * ppb + jnp.arange(ppb, dtype=jnp.int32)).reshape(DB, -1)
        phys = jnp.take_along_axis(pt_flat, logical, axis=1).reshape(-1)
        o_slot = _moba_sample_attend(phys, q_slot, k_new.reshape(DB, 1, KV_W), v_new.reshape(DB, 1, KV_W),
                                     cache_k_t, cache_v_t, l)
        o_slot = o_slot.reshape(DB, MOBA_HEADS, MOBA_KV_HEADS, MOBA_HEAD_DIM)
        attn_s = jnp.concatenate([o_slot[:, hh, hh // MOBA_GROUP] for hh in range(MOBA_HEADS)], axis=1)
        kc_new = jnp.concatenate([ckv[Tp:], krope[Tp:]], axis=1).reshape(DB, 1, MLA_QW)
        olat_s = _mla_sample(pt_flat, q_mla[Tp:].reshape(DB, MLA_HEADS, MLA_QW), kc_new,
                             cache_mla_ckv, cache_kr_t, l)

        attn_b = jnp.concatenate([attn_p, attn_s.astype(BF16)], axis=0)
        o_lat = jnp.concatenate([olat_p, olat_s.reshape(DB, MLA_HEADS * MLA_KV_LORA)], axis=0)
        wuv_t = jnp.transpose(mla_w_uv[l], (1, 0, 2)).astype(BF16)
        merged = _merge(um, attn_b, o_lat, gates, sgu_proj[l].astype(BF16), moba_proj[l].astype(BF16),
                        mla_proj[l].astype(BF16), wuv_t, tm=tm_merge)
        h = _out_proj(merged, mix_w_out[l].astype(BF16), h, row(mix_post_g[l]), tm=tm)

        h = _ffn_half(h, row(ffn2_pre_g[l]), ffn2_w_in[l].astype(BF16), ffn2_w_out[l].astype(BF16),
                      row(ffn2_post_g[l]), tm=tm, tf=tf)

        k_all = z[:, OFF_K:OFF_K + KV_W]
        v_all = z[:, OFF_V:OFF_V + KV_W]
        kr_all = krope[:, :MLA_ROPE]
        outs[0].append(k_all[:Tp].reshape(B, S, MOBA_KV_HEADS, MOBA_HEAD_DIM))
        outs[1].append(v_all[:Tp].reshape(B, S, MOBA_KV_HEADS, MOBA_HEAD_DIM))
        outs[2].append(ckv[:Tp].reshape(B, S, MLA_KV_LORA))
        outs[3].append(kr_all[:Tp].reshape(B, S, MLA_ROPE))
        outs[4].append(k_all[Tp:].reshape(DB, 1, MOBA_KV_HEADS, MOBA_HEAD_DIM))
        outs[5].append(v_all[Tp:].reshape(DB, 1, MOBA_KV_HEADS, MOBA_HEAD_DIM))
        outs[6].append(ckv[Tp:].reshape(DB, 1, MLA_KV_LORA))
        outs[7].append(kr_all[Tp:].reshape(DB, 1, MLA_ROPE))
        outs[8].append(v_rows[Tp:].reshape(DB, 1, A_WIDTH))

    return (h[:Tp].reshape(B, S, D), h[Tp:].reshape(DB, 1, D)) + tuple(jnp.stack(o) for o in outs)
```
